```python
import jax, jax.numpy as jnp
from jax import lax
import numpy as np

D_MODEL = 2048
BATCH = 2
SEQ = 4096
DEPTH = 4

D_MIX = D_MODEL
ATTN_HEADS = 16
ATTN_KV_HEADS = 4
ATTN_HEAD_DIM = 64
IDX_HEADS = 8
IDX_DIM = 64
IDX_TOPK_MAX = 256
Q_BLOCK = 128
CONV_DIM = 512
CONV_WIDTH = 3
RET_HEADS = 4
RET_HEAD_DIM = 128
RET_CHUNK = 128
ROPE_BASE = 10000.0
N_EXPERTS = 32
TOP_K = 4
D_FF = 768
SWIGLU_LIMIT = 7.0
SWIGLU_ALPHA = 1.702
LN_EPS = 1e-5
DEEPNORM_ALPHA = (2 * DEPTH) ** 0.25
DEEPNORM_BETA = (8 * DEPTH) ** -0.25

ATTN_W = ATTN_HEADS * ATTN_HEAD_DIM
KV_W = ATTN_KV_HEADS * ATTN_HEAD_DIM
RET_W = RET_HEADS * RET_HEAD_DIM
IN_SIZES = (ATTN_W, KV_W, KV_W, IDX_HEADS * IDX_DIM, IDX_DIM, IDX_HEADS,
            CONV_DIM, CONV_DIM, CONV_DIM, RET_W, RET_W, RET_W, RET_W)
VALUE_SLOTS = (2, 8, 11)
N_IN = sum(IN_SIZES)

kernel_name = 'hybrid_dsa_conv_retention_moe_deepnorm'


def _split_points():
    pts, acc = [], 0
    for s in IN_SIZES[:-1]:
        acc += s
        pts.append(acc)
    return pts


def layer_norm(x, g, b):
    xf = x.astype(jnp.float32)
    mu = jnp.mean(xf, axis=-1, keepdims=True)
    var = jnp.mean(jnp.square(xf - mu), axis=-1, keepdims=True)
    y = (xf - mu) * lax.rsqrt(var + LN_EPS) * g.astype(jnp.float32) + b.astype(jnp.float32)
    return y.astype(x.dtype)


def dsa_attention(q, k, v, qi, ki, wi, topk):
    B, L = q.shape[:2]
    nb = L // Q_BLOCK
    rep = ATTN_HEADS // ATTN_KV_HEADS
    scale = ATTN_HEAD_DIM ** -0.5
    key_pos = jnp.arange(L)

    def to_blocks(a):
        return a.reshape(B, nb, Q_BLOCK, *a.shape[2:]).swapaxes(0, 1)

    def one_block(args):
        qb, qib, wib, start = args
        qpos = start + jnp.arange(Q_BLOCK)
        rel = jax.nn.relu(jnp.einsum('bqhd,bsd->bqhs', qib, ki))
        score = jnp.einsum('bqhs,bqh->bqs', rel, wib).astype(jnp.float32)
        causal = key_pos[None, :] <= qpos[:, None]
        score = jnp.where(causal[None], score, -jnp.inf)
        _, idx = lax.top_k(score, topk)
        valid = idx <= qpos[None, :, None]
        kg = jax.vmap(lambda kk, ii: kk[ii])(k, idx)
        vg = jax.vmap(lambda vv, ii: vv[ii])(v, idx)
        qg = qb.reshape(B, Q_BLOCK, ATTN_KV_HEADS, rep, ATTN_HEAD_DIM)
        s = jnp.einsum('bqgrd,bqkgd->bqgrk', qg, kg).astype(jnp.float32) * scale
        s = jnp.where(valid[:, :, None, None, :], s, -jnp.inf)
        p = jax.nn.softmax(s, axis=-1).astype(v.dtype)
        o = jnp.einsum('bqgrk,bqkgd->bqgrd', p, vg)
        return o.reshape(B, Q_BLOCK, ATTN_HEADS * ATTN_HEAD_DIM)

    starts = jnp.arange(nb, dtype=jnp.int32) * Q_BLOCK
    out = lax.map(one_block, (to_blocks(q), to_blocks(qi), to_blocks(wi), starts))
    return out.swapaxes(0, 1).reshape(B, L, ATTN_HEADS * ATTN_HEAD_DIM)


def short_conv(b_gate, c_gate, h, conv_w):
    u = c_gate * h
    y = lax.conv_general_dilated(u, conv_w[:, None, :], window_strides=(1,),
                                 padding=[(CONV_WIDTH - 1, 0)],
                                 dimension_numbers=('NWC', 'WIO', 'NWC'),
                                 feature_group_count=CONV_DIM)
    return b_gate * y


def rotary(x, pos):
    half = x.shape[-1] // 2
    inv_freq = ROPE_BASE ** (-jnp.linspace(0.0, 1.0, half, dtype=jnp.float32))
    ang = pos.astype(jnp.float32)[:, None] * inv_freq[None, :]
    cos = jnp.cos(ang)[None, :, None, :]
    sin = jnp.sin(ang)[None, :, None, :]
    xf = x.astype(jnp.float32)
    x1, x2 = xf[..., :half], xf[..., half:]
    return jnp.concatenate([x1 * cos - x2 * sin, x1 * sin + x2 * cos], axis=-1).astype(x.dtype)


def retention(q, k, v):
    B, L, H, dk = q.shape
    dv = v.shape[-1]
    dt = q.dtype
    nc = L // RET_CHUNK
    lg = jnp.log1p(-jnp.exp2(-5.0 - jnp.arange(H, dtype=jnp.float32)))
    i = jnp.arange(RET_CHUNK, dtype=jnp.float32)
    diff = i[:, None] - i[None, :]
    intra = jnp.where(diff[None] >= 0,
                      jnp.exp(jnp.maximum(diff, 0.0)[None] * lg[:, None, None]), 0.0).astype(dt)
    q_dec = jnp.exp((i + 1.0)[None, :] * lg[:, None]).astype(dt)[..., None]
    k_dec = jnp.exp((RET_CHUNK - 1.0 - i)[None, :] * lg[:, None]).astype(dt)[..., None]
    c_dec = jnp.exp(RET_CHUNK * lg).astype(dt)[:, None, None]

    def chunks(a):
        return a.reshape(B, nc, RET_CHUNK, H, a.shape[-1]).transpose(1, 0, 3, 2, 4)

    def step(state, inp):
        qc, kc, vc = inp
        s = jnp.einsum('bhid,bhjd->bhij', qc, kc) * intra
        o = (jnp.einsum('bhij,bhje->bhie', s, vc)
             + jnp.einsum('bhid,bhde->bhie', qc * q_dec, state))
        state = state * c_dec + jnp.einsum('bhjd,bhje->bhde', kc * k_dec, vc)
        return state, o

    state0 = jnp.zeros((B, H, dk, dv), dt)
    _, o = lax.scan(step, state0, (chunks(q), chunks(k), chunks(v)))
    return o.transpose(1, 0, 3, 2, 4).reshape(B, L, H, dv)


def head_group_norm(o):
    of = o.astype(jnp.float32)
    mu = jnp.mean(of, axis=-1, keepdims=True)
    var = jnp.mean(jnp.square(of - mu), axis=-1, keepdims=True)
    return ((of - mu) * lax.rsqrt(var + LN_EPS)).astype(o.dtype)


def hybrid_mixer(x, w_in, conv_w, w_out):
    B, L, _ = x.shape
    proj = x @ w_in
    (aq, ak, av, iq, ik, iw, cb, cc, ch, rq, rk, rv, rg) = jnp.split(proj, _split_points(), axis=-1)
    topk = min(IDX_TOPK_MAX, L // 4)
    attn = dsa_attention(aq.reshape(B, L, ATTN_HEADS, ATTN_HEAD_DIM),
                         ak.reshape(B, L, ATTN_KV_HEADS, ATTN_HEAD_DIM),
                         av.reshape(B, L, ATTN_KV_HEADS, ATTN_HEAD_DIM),
                         iq.reshape(B, L, IDX_HEADS, IDX_DIM), ik,
                         iw * (IDX_HEADS * IDX_DIM) ** -0.5, topk)
    conv = short_conv(cb, cc, ch, conv_w)
    pos = jnp.arange(L)
    rq = rotary(rq.reshape(B, L, RET_HEADS, RET_HEAD_DIM), pos)
    rk = rotary(rk.reshape(B, L, RET_HEADS, RET_HEAD_DIM), pos) * (RET_HEAD_DIM ** -0.5)
    ret = retention(rq, rk, rv.reshape(B, L, RET_HEADS, RET_HEAD_DIM))
    ret = head_group_norm(ret).reshape(B, L, RET_W) * jax.nn.silu(rg)
    return jnp.concatenate([attn, conv, ret], axis=-1) @ w_out


def expert_ffn(xt, w_gu, b_gu, w_down, b_down):
    h = xt @ w_gu + b_gu
    gate = jnp.minimum(h[..., ::2], SWIGLU_LIMIT)
    up = jnp.clip(h[..., 1::2], -SWIGLU_LIMIT, SWIGLU_LIMIT)
    glu = gate * jax.nn.sigmoid(SWIGLU_ALPHA * gate)
    return ((up + 1.0) * glu) @ w_down + b_down


def moe(x, router_w, router_b, w_gu, b_gu, w_down, b_down):
    B, L, D = x.shape
    xt = x.reshape(B * L, D)
    logits = (xt @ router_w + router_b).astype(jnp.float32)
    top_val, top_idx = lax.top_k(logits, TOP_K)
    gates = jax.nn.softmax(top_val, axis=-1)
    combine = jnp.einsum('tk,tke->te', gates,
                         jax.nn.one_hot(top_idx, N_EXPERTS, dtype=jnp.float32)).astype(x.dtype)
    out = jnp.zeros_like(xt)
    for e in range(N_EXPERTS):
        out = out + combine[:, e:e + 1] * expert_ffn(xt, w_gu[e], b_gu[e], w_down[e], b_down[e])
    return out.reshape(B, L, D)


def setup_inputs(seed: int = 0) -> dict:
    key = jax.random.key(seed)
    ks = jax.random.split(key, 16)
    f32 = jnp.float32
    x = jax.random.normal(ks[0], (BATCH, SEQ, D_MODEL), f32)
    col_scale = np.ones((N_IN,), np.float32)
    starts = [0] + _split_points()
    for slot in VALUE_SLOTS:
        col_scale[starts[slot]:starts[slot] + IN_SIZES[slot]] = DEEPNORM_BETA
    w_in = jax.random.normal(ks[1], (DEPTH, D_MODEL, N_IN), f32) * (D_MODEL ** -0.5) * jnp.asarray(col_scale)
    conv_w = jax.random.normal(ks[2], (DEPTH, CONV_WIDTH, CONV_DIM), f32) * (CONV_WIDTH ** -0.5)
    w_out = jax.random.normal(ks[3], (DEPTH, D_MIX, D_MODEL), f32) * (D_MIX ** -0.5 * DEEPNORM_BETA)
    ln1_g = 1.0 + 0.02 * jax.random.normal(ks[4], (DEPTH, D_MODEL), f32)
    ln1_b = 0.02 * jax.random.normal(ks[5], (DEPTH, D_MODEL), f32)
    router_w = jax.random.normal(ks[6], (DEPTH, D_MODEL, N_EXPERTS), f32) * (D_MODEL ** -0.5)
    router_b = 0.01 * jax.random.normal(ks[7], (DEPTH, N_EXPERTS), f32)
    w_gu = jax.random.normal(ks[8], (DEPTH, N_EXPERTS, D_MODEL, 2 * D_FF), f32) * (D_MODEL ** -0.5)
    b_gu = 0.01 * jax.random.normal(ks[9], (DEPTH, N_EXPERTS, 2 * D_FF), f32)
    w_down = jax.random.normal(ks[10], (DEPTH, N_EXPERTS, D_FF, D_MODEL), f32) * (D_FF ** -0.5 * DEEPNORM_BETA)
    b_down = 0.01 * jax.random.normal(ks[11], (DEPTH, N_EXPERTS, D_MODEL), f32)
    ln2_g = 1.0 + 0.02 * jax.random.normal(ks[12], (DEPTH, D_MODEL), f32)
    ln2_b = 0.02 * jax.random.normal(ks[13], (DEPTH, D_MODEL), f32)
    return {'x': x, 'w_in': w_in, 'conv_w': conv_w, 'w_out': w_out,
            'ln1_g': ln1_g, 'ln1_b': ln1_b, 'router_w': router_w, 'router_b': router_b,
            'w_gu': w_gu, 'b_gu': b_gu, 'w_down': w_down, 'b_down': b_down,
            'ln2_g': ln2_g, 'ln2_b': ln2_b}


def reference(x, w_in, conv_w, w_out, ln1_g, ln1_b, router_w, router_b,
              w_gu, b_gu, w_down, b_down, ln2_g, ln2_b):
    for l in range(DEPTH):
        mix = hybrid_mixer(x, w_in[l], conv_w[l], w_out[l])
        x = layer_norm(DEEPNORM_ALPHA * x + mix, ln1_g[l], ln1_b[l])
        ffn = moe(x, router_w[l], router_b[l], w_gu[l], b_gu[l], w_down[l], b_down[l])
        x = layer_norm(DEEPNORM_ALPHA * x + ffn, ln2_g[l], ln2_b[l])
    return x
```

```python
import functools

import jax
import jax.numpy as jnp
import numpy as np
from jax import lax
from jax.experimental import pallas as pl
from jax.experimental.pallas import tpu as pltpu

F32 = jnp.float32
BF16 = jnp.bfloat16
I32 = jnp.int32

ATTN_HEADS = 16
ATTN_KV_HEADS = 4
HEAD_DIM = 64
IDX_HEADS = 8
IDX_DIM = 64
IDX_TOPK_MAX = 256
Q_BLOCK = 128
CONV_DIM = 512
CONV_WIDTH = 3
RET_HEADS = 4
RET_HEAD_DIM = 128
RET_CHUNK = 128
ROPE_BASE = 10000.0
N_EXPERTS = 32
TOP_K = 4
D_FF = 768
SWIGLU_LIMIT = 7.0
SWIGLU_ALPHA = 1.702
LN_EPS = 1e-5

ATTN_W = ATTN_HEADS * HEAD_DIM
KV_W = ATTN_KV_HEADS * HEAD_DIM
IDXQ_W = IDX_HEADS * IDX_DIM
RET_W = RET_HEADS * RET_HEAD_DIM
IN_SIZES = (ATTN_W, KV_W, KV_W, IDXQ_W, IDX_DIM, IDX_HEADS,
            CONV_DIM, CONV_DIM, CONV_DIM, RET_W, RET_W, RET_W, RET_W)

LANES = 128
KEY_CHUNK = 512
SEQ_TILE = 512
ROW_TILE = 256
EXPERT_TILE = 256
VMEM_LIMIT = 56 * 1024 * 1024

INT_MIN = -2 ** 31
NEG_BIG = -1e30


def _cparams(n_axes):
    return pltpu.CompilerParams(dimension_semantics=("arbitrary",) * n_axes,
                                vmem_limit_bytes=VMEM_LIMIT)


def _mm_kernel(x_ref, w_ref, o_ref):
    o_ref[...] = jnp.dot(x_ref[...].astype(BF16), w_ref[...],
                         preferred_element_type=F32).astype(o_ref.dtype)


def _matmul(x, w, tm, tn, out_dtype, name):
    m, k = x.shape
    n = w.shape[1]
    return pl.pallas_call(
        _mm_kernel,
        grid=(n // tn, m // tm),
        in_specs=[pl.BlockSpec((tm, k), lambda j, i: (i, 0)),
                  pl.BlockSpec((k, tn), lambda j, i: (0, j))],
        out_specs=pl.BlockSpec((tm, tn), lambda j, i: (i, j)),
        out_shape=jax.ShapeDtypeStruct((m, n), out_dtype),
        compiler_params=_cparams(2),
        name=name,
    )(x, w)


def _attn_kernel(q_ref, qi_ref, k_ref, vt_ref, ki_ref, iw_ref, o_ref,
                 qs_ref, qis_ref, s_ref, bias_ref, acc_ref, *, topk, kc):
    i = pl.program_id(1)
    q0 = i * Q_BLOCK
    nch = (q0 + Q_BLOCK + kc - 1) // kc
    nt = (((1,), (1,)), ((), ()))

    scale = HEAD_DIM ** -0.5
    for h in range(ATTN_HEADS):
        qs_ref[h * Q_BLOCK:(h + 1) * Q_BLOCK, :] = (
            q_ref[:, h * HEAD_DIM:(h + 1) * HEAD_DIM] * scale).astype(BF16)
    for h in range(IDX_HEADS):
        qis_ref[h * Q_BLOCK:(h + 1) * Q_BLOCK, :] = qi_ref[:, h * IDX_DIM:(h + 1) * IDX_DIM]

    wt = jnp.transpose(iw_ref[...])[IDX_DIM:IDX_DIM + IDX_HEADS, :] * (IDXQ_W ** -0.5)
    qpos = q0 + lax.broadcasted_iota(I32, (kc, LANES), 1)
    krow = lax.broadcasted_iota(I32, (kc, LANES), 0)

    def score_chunk(c, carry):
        r0 = pl.multiple_of(c * kc, kc)
        ki_c = ki_ref[pl.ds(r0, kc), :][:, :IDX_DIM].astype(BF16)
        r = lax.dot_general(ki_c, qis_ref[...], nt, preferred_element_type=F32)
        sc = jnp.zeros((kc, LANES), F32)
        for h in range(IDX_HEADS):
            sc = sc + jnp.maximum(r[:, h * LANES:(h + 1) * LANES], 0.0) * wt[h:h + 1, :]
        sc = jnp.where(sc == 0.0, 0.0, sc)
        bits = pltpu.bitcast(sc, I32)
        key = bits ^ ((bits >> 31) & 0x7FFFFFFF)
        s_ref[pl.ds(r0, kc), :] = jnp.where(r0 + krow <= qpos, key, INT_MIN)
        return carry

    lax.fori_loop(0, nch, score_chunk, 0)

    def count(pred):
        def body(c, acc):
            r0 = pl.multiple_of(c * kc, kc)
            m = jnp.where(pred(s_ref[pl.ds(r0, kc), :]), 1, 0)
            return acc + jnp.sum(m.reshape(kc // 8, 8, LANES), axis=0)
        acc = lax.fori_loop(0, nch, body, jnp.zeros((8, LANES), I32))
        return jnp.sum(acc, axis=0, keepdims=True)

    def bit_pass(b, res_u):
        cand_u = res_u | lax.shift_left(jnp.int32(1), 31 - b)
        cand_s = cand_u ^ INT_MIN
        cnt = count(lambda s: s >= cand_s)
        return jnp.where(cnt >= topk, cand_u, res_u)

    res_u = lax.fori_loop(0, 32, bit_pass, jnp.zeros((1, LANES), I32))
    thr = res_u ^ INT_MIN
    cnt_gt = count(lambda s: s > thr)
    cnt_ge = count(lambda s: s >= thr)
    need = topk - cnt_gt
    has_ties = jnp.max(jnp.where((cnt_ge - cnt_gt > need) & (thr != INT_MIN), 1, 0)) > 0

    @pl.when(jnp.logical_not(has_ties))
    def _():
        def body(c, carry):
            r0 = pl.multiple_of(c * kc, kc)
            s = s_ref[pl.ds(r0, kc), :]
            sel = (s >= thr) & (s != INT_MIN)
            bias_ref[pl.ds(r0, kc), :] = jnp.where(sel, 0.0, NEG_BIG)
            return carry
        lax.fori_loop(0, nch, body, 0)

    @pl.when(has_ties)
    def _():
        tri = (lax.broadcasted_iota(I32, (kc, kc), 1)
               < lax.broadcasted_iota(I32, (kc, kc), 0)).astype(BF16)
        need_f = need.astype(F32)

        def body(c, seen):
            r0 = pl.multiple_of(c * kc, kc)
            s = s_ref[pl.ds(r0, kc), :]
            eq = s == thr
            eq_f = jnp.where(eq, 1.0, 0.0)
            before = jnp.dot(tri, eq_f.astype(BF16), preferred_element_type=F32) + seen
            sel = ((s > thr) | (eq & (before < need_f))) & (s != INT_MIN)
            bias_ref[pl.ds(r0, kc), :] = jnp.where(sel, 0.0, NEG_BIG)
            return seen + jnp.sum(eq_f, axis=0, keepdims=True)
        lax.fori_loop(0, nch, body, jnp.zeros((1, LANES), F32))

    acc_ref[...] = jnp.zeros_like(acc_ref)
    rep = ATTN_HEADS // ATTN_KV_HEADS

    def attn_chunk(c, carry):
        ms, ls = carry
        r0 = pl.multiple_of(c * kc, kc)
        bias = bias_ref[pl.ds(r0, kc), :]
        k_c = k_ref[pl.ds(r0, kc), :]
        vt_c = vt_ref[:, pl.ds(r0, kc)]
        new_ms, new_ls = [], []
        for g in range(ATTN_KV_HEADS):
            k_g = k_c[:, g * HEAD_DIM:(g + 1) * HEAD_DIM]
            vt_g = vt_c[g * HEAD_DIM:(g + 1) * HEAD_DIM, :]
            qs_g = qs_ref[g * rep * Q_BLOCK:(g + 1) * rep * Q_BLOCK, :]
            st = lax.dot_general(k_g, qs_g, nt, preferred_element_type=F32)
            for r in range(rep):
                h = g * rep + r
                s = st[:, r * LANES:(r + 1) * LANES] + bias
                m_new = jnp.maximum(ms[h], jnp.max(s, axis=0, keepdims=True))
                alpha = jnp.exp(ms[h] - m_new)
                p = jnp.exp(s - m_new)
                new_ls.append(alpha * ls[h] + jnp.sum(p, axis=0, keepdims=True))
                new_ms.append(m_new)
                pv = jnp.dot(vt_g, p.astype(BF16), preferred_element_type=F32)
                rows = slice(h * HEAD_DIM, (h + 1) * HEAD_DIM)
                acc_ref[rows, :] = alpha * acc_ref[rows, :] + pv
        return tuple(new_ms), tuple(new_ls)

    init = (tuple(jnp.full((1, LANES), NEG_BIG, F32) for _ in range(ATTN_HEADS)),
            tuple(jnp.zeros((1, LANES), F32) for _ in range(ATTN_HEADS)))
    _, ls = lax.fori_loop(0, nch, attn_chunk, init)

    for h in range(ATTN_HEADS):
        rows = slice(h * HEAD_DIM, (h + 1) * HEAD_DIM)
        acc_ref[rows, :] = acc_ref[rows, :] / ls[h]
    o_ref[...] = jnp.transpose(acc_ref[...]).astype(o_ref.dtype)


def _attention(a_pack, i_pack, vt, batch, seq, topk):
    nqb = seq // Q_BLOCK
    kc = min(KEY_CHUNK, seq)
    kern = functools.partial(_attn_kernel, topk=topk, kc=kc)
    return pl.pallas_call(
        kern,
        grid=(batch, nqb),
        in_specs=[
            pl.BlockSpec((Q_BLOCK, ATTN_W), lambda b, i: (b * nqb + i, 0)),
            pl.BlockSpec((Q_BLOCK, IDXQ_W), lambda b, i: (b * nqb + i, (ATTN_W + 2 * KV_W) // IDXQ_W)),
            pl.BlockSpec((seq, KV_W), lambda b, i: (b, ATTN_W // KV_W)),
            pl.BlockSpec((None, KV_W, seq), lambda b, i: (b, 0, 0)),
            pl.BlockSpec((seq, LANES), lambda b, i: (b, 0)),
            pl.BlockSpec((Q_BLOCK, LANES), lambda b, i: (b * nqb + i, 0)),
        ],
        out_specs=pl.BlockSpec((Q_BLOCK, ATTN_W), lambda b, i: (b * nqb + i, 0)),
        out_shape=jax.ShapeDtypeStruct((batch * seq, ATTN_W), BF16),
        scratch_shapes=[
            pltpu.VMEM((ATTN_HEADS * Q_BLOCK, HEAD_DIM), BF16),
            pltpu.VMEM((IDX_HEADS * Q_BLOCK, IDX_DIM), BF16),
            pltpu.VMEM((seq, LANES), I32),
            pltpu.VMEM((seq, LANES), F32),
            pltpu.VMEM((ATTN_W, LANES), F32),
        ],
        compiler_params=_cparams(2),
        name="dsa_attention",
    )(a_pack, a_pack, a_pack, vt, i_pack, i_pack)


def _retconv_kernel(cb_ref, cc_ref, ch_ref, rq_ref, rk_ref, rv_ref, rg_ref,
                    cw_ref, cos_ref, sin_ref, dec_ref, conv_o, ret_o,
                    halo_ref, state_ref, *, ts):
    j = pl.program_id(1)

    @pl.when(j == 0)
    def _():
        halo_ref[...] = jnp.zeros_like(halo_ref)
        state_ref[...] = jnp.zeros_like(state_ref)

    u = cc_ref[...] * ch_ref[...]
    row = lax.broadcasted_iota(I32, u.shape, 0)
    prev1 = halo_ref[7:8, :]
    prev2 = halo_ref[6:7, :]
    u1 = jnp.where(row == 0, prev1, pltpu.roll(u, 1, 0))
    u2 = jnp.where(row == 0, prev2, jnp.where(row == 1, prev1, pltpu.roll(u, 2, 0)))
    y = cw_ref[0:1, :] * u2 + cw_ref[1:2, :] * u1 + cw_ref[2:3, :] * u
    conv_o[...] = (cb_ref[...] * y).astype(conv_o.dtype)
    halo_ref[...] = u[ts - 8:ts, :]

    cos = cos_ref[...]
    sin = sin_ref[...]
    half = RET_HEAD_DIM // 2
    nt = (((1,), (1,)), ((), ()))
    tn = (((0,), (0,)), ((), ()))
    for h in range(RET_HEADS):
        cols = slice(h * RET_HEAD_DIM, (h + 1) * RET_HEAD_DIM)
        qh = rq_ref[:, cols]
        kh = rk_ref[:, cols]
        qh = qh * cos + pltpu.roll(qh, half, 1) * sin
        kh = (kh * cos + pltpu.roll(kh, half, 1) * sin) * (RET_HEAD_DIM ** -0.5)
        vh = rv_ref[:, cols]
        intra = dec_ref[0, h]
        q_dec = dec_ref[1, h]
        k_dec = dec_ref[2, h]
        c_dec = dec_ref[3, h]
        state = state_ref[h]
        outs = []
        for c in range(ts // RET_CHUNK):
            rows = slice(c * RET_CHUNK, (c + 1) * RET_CHUNK)
            qc, kc_, vc = qh[rows], kh[rows], vh[rows].astype(BF16)
            s = lax.dot_general(qc.astype(BF16), kc_.astype(BF16), nt,
                                preferred_element_type=F32) * intra
            o = (jnp.dot(s.astype(BF16), vc, preferred_element_type=F32)
                 + jnp.dot((qc * q_dec).astype(BF16), state.astype(BF16),
                           preferred_element_type=F32))
            state = state * c_dec + lax.dot_general((kc_ * k_dec).astype(BF16), vc, tn,
                                                    preferred_element_type=F32)
            outs.append(o)
        state_ref[h] = state
        o = jnp.concatenate(outs, axis=0)
        mu = jnp.mean(o, axis=-1, keepdims=True)
        d = o - mu
        var = jnp.mean(d * d, axis=-1, keepdims=True)
        on = d * lax.rsqrt(var + LN_EPS)
        g = rg_ref[:, cols]
        ret_o[:, cols] = (on * (g / (1.0 + jnp.exp(-g)))).astype(ret_o.dtype)


def _retconv(r_pack, conv_w, cos_t, sin_t, dec, batch, seq):
    ts = min(SEQ_TILE, seq)
    nst = seq // ts
    t = batch * seq
    kern = functools.partial(_retconv_kernel, ts=ts)
    specs = [pl.BlockSpec((ts, CONV_DIM),
                          functools.partial(lambda b, j, n: (b * nst + j, n), n=n))
             for n in range(7)]
    return pl.pallas_call(
        kern,
        grid=(batch, nst),
        in_specs=specs + [
            pl.BlockSpec((8, CONV_DIM), lambda b, j: (0, 0)),
            pl.BlockSpec((ts, RET_HEAD_DIM), lambda b, j: (j, 0)),
            pl.BlockSpec((ts, RET_HEAD_DIM), lambda b, j: (j, 0)),
            pl.BlockSpec((4, RET_HEADS, RET_CHUNK, RET_CHUNK), lambda b, j: (0, 0, 0, 0)),
        ],
        out_specs=[pl.BlockSpec((ts, CONV_DIM), lambda b, j: (b * nst + j, 0)),
                   pl.BlockSpec((ts, RET_W), lambda b, j: (b * nst + j, 0))],
        out_shape=[jax.ShapeDtypeStruct((t, CONV_DIM), BF16),
                   jax.ShapeDtypeStruct((t, RET_W), BF16)],
        scratch_shapes=[pltpu.VMEM((8, CONV_DIM), F32),
                        pltpu.VMEM((RET_HEADS, RET_HEAD_DIM, RET_HEAD_DIM), F32)],
        compiler_params=_cparams(2),
        name="conv_retention",
    )(*([r_pack] * 7), conv_w, cos_t, sin_t, dec)


def _layer_norm(y, g, b):
    mu = jnp.mean(y, axis=-1, keepdims=True)
    d = y - mu
    var = jnp.mean(d * d, axis=-1, keepdims=True)
    return d * lax.rsqrt(var + LN_EPS) * g + b


def _outproj_kernel(attn_ref, conv_ref, ret_ref, x_ref, w_ref, g_ref, b_ref, rw_ref, rb_ref,
                    x1_ref, idx_ref, gate_ref, *, alpha):
    mix = jnp.dot(attn_ref[...], w_ref[0:ATTN_W, :], preferred_element_type=F32)
    mix = mix + jnp.dot(conv_ref[...], w_ref[ATTN_W:ATTN_W + CONV_DIM, :],
                        preferred_element_type=F32)
    mix = mix + jnp.dot(ret_ref[...], w_ref[ATTN_W + CONV_DIM:, :], preferred_element_type=F32)
    x1 = _layer_norm(alpha * x_ref[...] + mix, g_ref[...], b_ref[...])
    x1_ref[...] = x1

    nt = (((1,), (1,)), ((), ()))
    logits = lax.dot_general(rw_ref[...], x1.astype(BF16), nt,
                             preferred_element_type=F32) + rb_ref[...]
    eidx = lax.broadcasted_iota(I32, logits.shape, 0)
    vals, idxs = [], []
    for _ in range(TOP_K):
        mx = jnp.max(logits, axis=0, keepdims=True)
        am = jnp.min(jnp.where(logits == mx, eidx, N_EXPERTS), axis=0, keepdims=True)
        vals.append(mx)
        idxs.append(am)
        logits = jnp.where(eidx == am, -jnp.inf, logits)
    v = jnp.concatenate(vals, axis=0)
    e = jnp.exp(v - v[0:1, :])
    gate_ref[...] = e / jnp.sum(e, axis=0, keepdims=True)
    idx_ref[...] = jnp.concatenate(idxs, axis=0)


def _outproj_ln_router(attn, conv, ret, x, w_out, g, b, rw_t, rb, alpha):
    t, d = x.shape
    tm = min(ROW_TILE, t)
    kern = functools.partial(_outproj_kernel, alpha=alpha)
    row = lambda w: pl.BlockSpec((tm, w), lambda i: (i, 0))
    full = lambda s: pl.BlockSpec(s, lambda i: (0,) * len(s))
    return pl.pallas_call(
        kern,
        grid=(t // tm,),
        in_specs=[row(ATTN_W), row(CONV_DIM), row(RET_W), row(d), full((d, d)),
                  full((1, d)), full((1, d)), full((N_EXPERTS, d)), full((N_EXPERTS, 1))],
        out_specs=[row(d), pl.BlockSpec((TOP_K, tm), lambda i: (0, i)),
                   pl.BlockSpec((TOP_K, tm), lambda i: (0, i))],
        out_shape=[jax.ShapeDtypeStruct((t, d), F32),
                   jax.ShapeDtypeStruct((TOP_K, t), I32),
                   jax.ShapeDtypeStruct((TOP_K, t), F32)],
        compiler_params=_cparams(1),
        name="outproj_ln_router",
    )(attn, conv, ret, x, w_out, g, b, rw_t, rb)


def _expert_kernel(te_ref, tv_ref, nu_ref, tokc_ref, tokn_ref, dst_ref, gate_ref, x_hbm,
                   wg_ref, wu_ref, wd_ref, bg_ref, bu_ref, bd_ref, out_hbm,
                   xbuf, ybuf, gsem, ssem, *, tm):
    i = pl.program_id(0)
    n_used = nu_ref[0]
    slot = lax.rem(i, 2)

    def gather_row(tok_ref, r, s):
        return pltpu.make_async_copy(x_hbm.at[pl.ds(tok_ref[0, 0, r], 1)],
                                     xbuf.at[s, pl.ds(r, 1)], gsem.at[s])

    def scatter_row(dst, r, s):
        return pltpu.make_async_copy(ybuf.at[s, pl.ds(r, 1)],
                                     out_hbm.at[pl.ds(dst, 1)], ssem.at[s])

    def start_gather(tok_ref, s):
        def body(r, carry):
            gather_row(tok_ref, r, s).start()
            return carry
        lax.fori_loop(0, tm, body, 0)

    def wait_gather(tok_ref, s):
        def body(r, carry):
            gather_row(tok_ref, r, s).wait()
            return carry
        lax.fori_loop(0, tm, body, 0)

    def wait_scatter(step, s):
        def body(r, carry):
            scatter_row(0, r, s).wait()
            return carry
        lax.fori_loop(0, tv_ref[step], body, 0)

    @pl.when((i == 0) & (n_used > 0))
    def _():
        start_gather(tokc_ref, 0)

    @pl.when(i + 1 < n_used)
    def _():
        start_gather(tokn_ref, 1 - slot)

    @pl.when(i < n_used)
    def _():
        wait_gather(tokc_ref, slot)
        x = xbuf[slot].astype(BF16)
        gt = jnp.dot(x, wg_ref[...], preferred_element_type=F32) + bg_ref[...]
        up = jnp.dot(x, wu_ref[...], preferred_element_type=F32) + bu_ref[...]
        gt = jnp.minimum(gt, SWIGLU_LIMIT)
        up = jnp.clip(up, -SWIGLU_LIMIT, SWIGLU_LIMIT)
        act = (up + 1.0) * (gt / (1.0 + jnp.exp(-SWIGLU_ALPHA * gt)))
        y = jnp.dot(act.astype(BF16), wd_ref[...], preferred_element_type=F32) + bd_ref[...]
        ybuf[slot] = y * gate_ref[...]

        def body(r, carry):
            scatter_row(dst_ref[0, 0, r], r, slot).start()
            return carry
        lax.fori_loop(0, tv_ref[i], body, 0)

        @pl.when(i >= 1)
        def _():
            wait_scatter(i - 1, 1 - slot)

        @pl.when(i == n_used - 1)
        def _():
            wait_scatter(i, slot)


def _experts(x1, tile_expert, tile_valid, n_used, row_tok, row_dst, row_gate,
             wg, wu, wd, bg, bu, bd, layer, tm):
    t, d = x1.shape
    n_tiles = tile_expert.shape[0]
    kern = functools.partial(_expert_kernel, tm=tm)
    smem_blk = lambda f: pl.BlockSpec((1, 1, tm), f, memory_space=pltpu.SMEM)
    wspec = lambda s: pl.BlockSpec((None, None) + s,
                                   lambda i, te, tv, nu: (layer, te[i], 0, 0))
    grid_spec = pltpu.PrefetchScalarGridSpec(
        num_scalar_prefetch=3,
        grid=(n_tiles,),
        in_specs=[
            smem_blk(lambda i, te, tv, nu: (i, 0, 0)),
            smem_blk(lambda i, te, tv, nu: (jnp.minimum(i + 1, n_tiles - 1), 0, 0)),
            smem_blk(lambda i, te, tv, nu: (i, 0, 0)),
            pl.BlockSpec((tm, 1), lambda i, te, tv, nu: (i, 0)),
            pl.BlockSpec(memory_space=pl.ANY),
            wspec((d, D_FF)), wspec((d, D_FF)), wspec((D_FF, d)),
            wspec((1, D_FF)), wspec((1, D_FF)), wspec((1, d)),
        ],
        out_specs=pl.BlockSpec(memory_space=pl.ANY),
        scratch_shapes=[pltpu.VMEM((2, tm, d), F32), pltpu.VMEM((2, tm, d), F32),
                        pltpu.SemaphoreType.DMA((2,)), pltpu.SemaphoreType.DMA((2,))],
    )
    return pl.pallas_call(
        kern,
        grid_spec=grid_spec,
        out_shape=jax.ShapeDtypeStruct((t * TOP_K, d), F32),
        compiler_params=_cparams(1),
        name="expert_ffn",
    )(tile_expert, tile_valid, n_used, row_tok, row_tok, row_dst, row_gate, x1,
      wg, wu, wd, bg, bu, bd)


def _route(top_idx, gates, tm):
    t = top_idx.shape[1]
    n_assign = t * TOP_K
    n_tiles = n_assign // tm + N_EXPERTS
    e_flat = top_idx.T.reshape(-1)
    g_flat = gates.T.reshape(-1)
    order = jnp.argsort(e_flat, stable=True).astype(I32)
    e_sorted = e_flat[order]
    counts = jnp.zeros((N_EXPERTS,), I32).at[e_flat].add(1)
    padded = ((counts + tm - 1) // tm) * tm
    start = jnp.cumsum(counts) - counts
    pstart = jnp.cumsum(padded) - padded
    prow = pstart[e_sorted] + (jnp.arange(n_assign, dtype=I32) - start[e_sorted])
    rows = n_tiles * tm
    row_tok = jnp.zeros((rows,), I32).at[prow].set(order // TOP_K)
    row_dst = jnp.zeros((rows,), I32).at[prow].set(order)
    row_gate = jnp.zeros((rows,), F32).at[prow].set(g_flat[order])
    n_used = (jnp.sum(padded) // tm).astype(I32)
    tile_start = jnp.arange(n_tiles, dtype=I32) * tm
    pend = jnp.cumsum(padded)
    te = jnp.searchsorted(pend, tile_start, side="right").astype(I32)
    last = jnp.searchsorted(pend, (n_used - 1) * tm, side="right").astype(I32)
    te = jnp.minimum(te, last)
    tv = jnp.clip(counts[te] - (tile_start - pstart[te]), 0, tm).astype(I32)
    tv = jnp.where(jnp.arange(n_tiles) < n_used, tv, 0)
    return (te, tv, n_used.reshape(1), row_tok.reshape(n_tiles, 1, tm),
            row_dst.reshape(n_tiles, 1, tm), row_gate.reshape(rows, 1))


def _combine_kernel(y_ref, x_ref, g_ref, b_ref, o_ref, *, alpha, d):
    ffn = y_ref[:, 0:d]
    for k in range(1, TOP_K):
        ffn = ffn + y_ref[:, k * d:(k + 1) * d]
    o_ref[...] = _layer_norm(alpha * x_ref[...] + ffn, g_ref[...], b_ref[...])


def _combine_ln(y4, x1, g, b, alpha):
    t, d = x1.shape
    tm = min(ROW_TILE, t)
    kern = functools.partial(_combine_kernel, alpha=alpha, d=d)
    return pl.pallas_call(
        kern,
        grid=(t // tm,),
        in_specs=[pl.BlockSpec((tm, TOP_K * d), lambda i: (i, 0)),
                  pl.BlockSpec((tm, d), lambda i: (i, 0)),
                  pl.BlockSpec((1, d), lambda i: (0, 0)),
                  pl.BlockSpec((1, d), lambda i: (0, 0))],
        out_specs=pl.BlockSpec((tm, d), lambda i: (i, 0)),
        out_shape=jax.ShapeDtypeStruct((t, d), F32),
        compiler_params=_cparams(1),
        name="combine_ln",
    )(y4, x1, g, b)


def _rotary_tables(seq):
    half = RET_HEAD_DIM // 2
    inv_freq = ROPE_BASE ** (-jnp.linspace(0.0, 1.0, half, dtype=F32))
    ang = jnp.arange(seq).astype(F32)[:, None] * inv_freq[None, :]
    cos, sin = jnp.cos(ang), jnp.sin(ang)
    return (jnp.concatenate([cos, cos], axis=-1), jnp.concatenate([-sin, sin], axis=-1))


def _decay_tables():
    c = RET_CHUNK
    lg = jnp.log1p(-jnp.exp2(-5.0 - jnp.arange(RET_HEADS, dtype=F32)))
    i = jnp.arange(c, dtype=F32)
    diff = i[:, None] - i[None, :]
    intra = jnp.where(diff[None] >= 0, jnp.exp(jnp.maximum(diff, 0.0)[None] * lg[:, None, None]), 0.0)
    q_dec = jnp.exp((i + 1.0)[None, :] * lg[:, None])[..., None]
    k_dec = jnp.exp((c - 1.0 - i)[None, :] * lg[:, None])[..., None]
    c_dec = jnp.exp(c * lg)[:, None, None]
    full = (RET_HEADS, c, c)
    return jnp.stack([intra.astype(F32), jnp.broadcast_to(q_dec, full),
                      jnp.broadcast_to(k_dec, full), jnp.broadcast_to(c_dec, full)])


def _split_w_in(w_in):
    pts = np.cumsum((0,) + IN_SIZES)
    a = w_in[..., pts[0]:pts[4]]
    idx = w_in[..., pts[4]:pts[6]]
    idx = jnp.pad(idx, ((0, 0), (0, 0), (0, LANES - idx.shape[-1])))
    rest = w_in[..., pts[6]:]
    return a.astype(BF16), idx.astype(BF16), rest.astype(BF16)


def kernel(x, w_in, conv_w, w_out, ln1_g, ln1_b, router_w, router_b,
           w_gu, b_gu, w_down, b_down, ln2_g, ln2_b):
    batch, seq, d = x.shape
    depth = w_in.shape[0]
    t = batch * seq
    alpha = (2 * depth) ** 0.25
    topk = min(IDX_TOPK_MAX, seq // 4)
    tm_e = EXPERT_TILE

    wa, wi, wr = _split_w_in(w_in)
    w_out_b = w_out.astype(BF16)
    rw_t = jnp.swapaxes(router_w, 1, 2).astype(BF16)
    wg = w_gu[..., 0::2].astype(BF16)
    wu = w_gu[..., 1::2].astype(BF16)
    wd = w_down.astype(BF16)
    bg = b_gu[..., None, 0::2]
    bu = b_gu[..., None, 1::2]
    bd = b_down[..., None, :]
    cw = jnp.pad(conv_w, ((0, 0), (0, 8 - CONV_WIDTH), (0, 0)))
    cos_t, sin_t = _rotary_tables(seq)
    dec = _decay_tables()

    xf = x.reshape(t, d)
    for l in range(depth):
        a_pack = _matmul(xf, wa[l], min(512, t), 1024, BF16, "in_proj_attn")
        i_pack = _matmul(xf, wi[l], min(512, t), LANES, F32, "in_proj_idx")
        r_pack = _matmul(xf, wr[l], min(ROW_TILE, t), wr.shape[-1], F32, "in_proj_rest")
        vt = jnp.swapaxes(a_pack[:, ATTN_W + KV_W:ATTN_W + 2 * KV_W].reshape(batch, seq, KV_W), 1, 2)
        attn = _attention(a_pack, i_pack, vt, batch, seq, topk)
        conv, ret = _retconv(r_pack, cw[l], cos_t, sin_t, dec, batch, seq)
        x1, top_idx, gates = _outproj_ln_router(
            attn, conv, ret, xf, w_out_b[l], ln1_g[l][None], ln1_b[l][None],
            rw_t[l], router_b[l][:, None], alpha)
        te, tv, n_used, row_tok, row_dst, row_gate = _route(top_idx, gates, tm_e)
        y4 = _experts(x1, te, tv, n_used, row_tok, row_dst, row_gate,
                      wg, wu, wd, bg, bu, bd, l, tm_e)
        xf = _combine_ln(y4.reshape(t, TOP_K * d), x1, ln2_g[l][None], ln2_b[l][None], alpha)
    return xf.reshape(batch, seq, d)
```

```python
import functools

import jax
import jax.numpy as jnp
import numpy as np
from jax import lax
from jax.experimental import pallas as pl
from jax.experimental.pallas import tpu as pltpu

F32 = jnp.float32
BF16 = jnp.bfloat16
I32 = jnp.int32

ATTN_HEADS = 16
ATTN_KV_HEADS = 4
HEAD_DIM = 64
IDX_HEADS = 8
IDX_DIM = 64
IDX_TOPK_MAX = 256
Q_BLOCK = 128
CONV_DIM = 512
CONV_WIDTH = 3
RET_HEADS = 4
RET_HEAD_DIM = 128
RET_CHUNK = 128
ROPE_BASE = 10000.0
N_EXPERTS = 32
TOP_K = 4
D_FF = 768
SWIGLU_LIMIT = 7.0
SWIGLU_ALPHA = 1.702
LN_EPS = 1e-5

ATTN_W = ATTN_HEADS * HEAD_DIM
KV_W = ATTN_KV_HEADS * HEAD_DIM
IDXQ_W = IDX_HEADS * IDX_DIM
RET_W = RET_HEADS * RET_HEAD_DIM
IN_SIZES = (ATTN_W, KV_W, KV_W, IDXQ_W, IDX_DIM, IDX_HEADS,
            CONV_DIM, CONV_DIM, CONV_DIM, RET_W, RET_W, RET_W, RET_W)

LANES = 128
MXU_TILE = 256
BF16_ROWS = 16
KEY_CHUNK = 256
SEQ_TILE = 512
ROW_TILE = 256
EXPERT_TILE = 256
VMEM_LIMIT = 56 * 1024 * 1024

NEG_BIG = -1e30
V_ROWS = HEAD_DIM + BF16_ROWS
BISECT_STEPS = 20
BISECT_EXTRA = 4
BISECT_MAX_ROUNDS = 80


def _cparams(n_axes):
    return pltpu.CompilerParams(dimension_semantics=("arbitrary",) * n_axes,
                                vmem_limit_bytes=VMEM_LIMIT)


def _mm_kernel(x_ref, w_ref, o_ref):
    o_ref[...] = jnp.dot(x_ref[...].astype(BF16), w_ref[...],
                         preferred_element_type=F32).astype(o_ref.dtype)


def _matmul(x, w, tm, tn, out_dtype, name):
    m, k = x.shape
    n = w.shape[1]
    return pl.pallas_call(
        _mm_kernel,
        grid=(n // tn, m // tm),
        in_specs=[pl.BlockSpec((tm, k), lambda j, i: (i, 0)),
                  pl.BlockSpec((k, tn), lambda j, i: (0, j))],
        out_specs=pl.BlockSpec((tm, tn), lambda j, i: (i, j)),
        out_shape=jax.ShapeDtypeStruct((m, n), out_dtype),
        compiler_params=_cparams(2),
        name=name,
    )(x, w)


def _attn_kernel(q_ref, qi_ref, k_ref, vt_ref, ki_ref, iw_ref, o_ref,
                 qs_ref, qis_ref, s_ref, bias_ref, acc_ref, ot_ref, *, topk, kc):
    i = pl.program_id(1)
    q0 = i * Q_BLOCK
    nch = (q0 + Q_BLOCK + kc - 1) // kc
    kc2 = 2 * kc
    nch2 = (nch + 1) // 2
    nt = (((1,), (1,)), ((), ()))

    scale = HEAD_DIM ** -0.5
    for h in range(ATTN_HEADS):
        qs_ref[h * Q_BLOCK:(h + 1) * Q_BLOCK, :] = (
            q_ref[:, h * HEAD_DIM:(h + 1) * HEAD_DIM] * scale).astype(BF16)
    for h in range(IDX_HEADS):
        qis_ref[h * Q_BLOCK:(h + 1) * Q_BLOCK, :] = qi_ref[:, h * IDX_DIM:(h + 1) * IDX_DIM]

    wt = jnp.transpose(iw_ref[...])[IDX_DIM:IDX_DIM + IDX_HEADS, :] * (IDXQ_W ** -0.5)
    qpos = q0 + lax.broadcasted_iota(I32, (kc, LANES), 1)
    krow = lax.broadcasted_iota(I32, (kc, LANES), 0)

    def score_chunk(c, carry):
        lo, hi = carry
        for half in range(2):
            r0 = pl.multiple_of(c * kc2 + half * kc, kc)
            ki_c = ki_ref[pl.ds(r0, kc), :][:, :IDX_DIM].astype(BF16)
            r = lax.dot_general(ki_c, qis_ref[...], nt, preferred_element_type=F32)
            sc = jnp.zeros((kc, LANES), F32)
            for h in range(IDX_HEADS):
                sc = sc + jnp.maximum(r[:, h * LANES:(h + 1) * LANES], 0.0) * wt[h:h + 1, :]
            causal = r0 + krow <= qpos
            s_ref[pl.ds(r0, kc), :] = jnp.where(causal, sc, -jnp.inf)
            lo = jnp.minimum(lo, jnp.min(jnp.where(causal, sc, jnp.inf), axis=0, keepdims=True))
            hi = jnp.maximum(hi, jnp.max(jnp.where(causal, sc, -jnp.inf), axis=0, keepdims=True))
        return lo, hi

    lo0, hi0 = lax.fori_loop(0, nch2, score_chunk,
                             (jnp.full((1, LANES), jnp.inf, F32), jnp.full((1, LANES), -jnp.inf, F32)))

    def column_sum(vals):
        acc = lax.fori_loop(0, nch2, lambda c, a: a + vals(s_ref[pl.ds(pl.multiple_of(c * kc2, kc2), kc2), :]),
                            jnp.zeros((8, LANES), I32))
        return jnp.sum(acc, axis=0, keepdims=True)

    def count(pred):
        return column_sum(lambda s: jnp.sum(jnp.where(pred(s), 1, 0).reshape(kc2 // 8, 8, LANES), axis=0))

    def bisect(_, carry):
        lo, hi = carry
        mid = lo + (hi - lo) * 0.5
        mid = jnp.where(mid == lo, hi, mid)
        ge = count(lambda s: s >= mid) >= topk
        return jnp.where(ge, mid, lo), jnp.where(ge, hi, mid)

    def settle(lo):
        def body(c, m):
            s = s_ref[pl.ds(pl.multiple_of(c * kc2, kc2), kc2), :]
            return jnp.minimum(m, jnp.min(jnp.where(s >= lo, s, jnp.inf).reshape(kc2 // 8, 8, LANES), axis=0))
        m = lax.fori_loop(0, nch2, body, jnp.full((8, LANES), jnp.inf, F32))
        thr = jnp.min(m, axis=0, keepdims=True)
        return thr, count(lambda s: s > thr)

    lo, hi = lax.fori_loop(0, BISECT_STEPS, bisect, (lo0, hi0))
    thr, cnt_gt = settle(lo)

    def unsettled(state):
        _, _, _, cnt_gt, rounds = state
        return (jnp.max(cnt_gt) >= topk) & (rounds < BISECT_MAX_ROUNDS)

    def refine(state):
        lo, hi, _, _, rounds = state
        lo, hi = lax.fori_loop(0, BISECT_EXTRA, bisect, (lo, hi))
        thr, cnt_gt = settle(lo)
        return lo, hi, thr, cnt_gt, rounds + 1

    _, _, thr, cnt_gt, _ = lax.while_loop(unsettled, refine, (lo, hi, thr, cnt_gt, jnp.int32(0)))
    cnt_eq = count(lambda s: s == thr)
    need = topk - cnt_gt
    has_ties = jnp.max(jnp.where(cnt_eq > need, 1, 0)) > 0

    @pl.when(jnp.logical_not(has_ties))
    def _():
        def body(c, carry):
            r0 = pl.multiple_of(c * kc, kc)
            s = s_ref[pl.ds(r0, kc), :]
            bias_ref[pl.ds(r0, kc), :] = jnp.where(s >= thr, 0.0, NEG_BIG)
            return carry
        lax.fori_loop(0, nch, body, 0)

    @pl.when(has_ties)
    def _():
        tri = (lax.broadcasted_iota(I32, (kc, kc), 1)
               < lax.broadcasted_iota(I32, (kc, kc), 0)).astype(BF16)
        need_f = need.astype(F32)

        def body(c, seen):
            r0 = pl.multiple_of(c * kc, kc)
            s = s_ref[pl.ds(r0, kc), :]
            eq = s == thr
            eq_f = jnp.where(eq, 1.0, 0.0)
            before = jnp.dot(tri, eq_f.astype(BF16), preferred_element_type=F32) + seen
            sel = (s > thr) | (eq & (before < need_f))
            bias_ref[pl.ds(r0, kc), :] = jnp.where(sel, 0.0, NEG_BIG)
            return seen + jnp.sum(eq_f, axis=0, keepdims=True)
        lax.fori_loop(0, nch, body, jnp.zeros((1, LANES), F32))

    acc_ref[...] = jnp.zeros_like(acc_ref)
    rep = ATTN_HEADS // ATTN_KV_HEADS

    def attn_chunk(c, ms):
        r0 = pl.multiple_of(c * kc, kc)
        bias = bias_ref[pl.ds(r0, kc), :]
        k_c = k_ref[pl.ds(r0, kc), :]
        new_ms = []
        sts = []
        for g in range(ATTN_KV_HEADS):
            k_g = k_c[:, g * HEAD_DIM:(g + 1) * HEAD_DIM]
            qs_g = qs_ref[g * rep * Q_BLOCK:(g + 1) * rep * Q_BLOCK, :]
            sts.append(lax.dot_general(k_g, qs_g, nt, preferred_element_type=F32))
        for g in range(ATTN_KV_HEADS):
            vt_g = vt_ref[g, :, pl.ds(r0, kc)]
            for r in range(rep):
                h = g * rep + r
                s = sts[g][:, r * LANES:(r + 1) * LANES] + bias
                m_new = jnp.maximum(ms[h], jnp.max(s, axis=0, keepdims=True))
                alpha = jnp.exp(ms[h] - m_new)
                p = jnp.exp(s - m_new).astype(BF16)
                new_ms.append(m_new)
                pv = jnp.dot(vt_g, p, preferred_element_type=F32)
                rows = slice(h * V_ROWS, (h + 1) * V_ROWS)
                acc_ref[rows, :] = alpha * acc_ref[rows, :] + pv
        return tuple(new_ms)

    init = tuple(jnp.full((1, LANES), NEG_BIG, F32) for _ in range(ATTN_HEADS))
    lax.fori_loop(0, nch, attn_chunk, init)

    for h in range(ATTN_HEADS):
        num = acc_ref[h * V_ROWS:h * V_ROWS + HEAD_DIM, :]
        den = acc_ref[h * V_ROWS + HEAD_DIM:h * V_ROWS + HEAD_DIM + 1, :]
        ot_ref[h * HEAD_DIM:(h + 1) * HEAD_DIM, :] = num / den
    o_ref[...] = jnp.transpose(ot_ref[...]).astype(o_ref.dtype)


def _attention(a_pack, i_pack, vt, batch, seq, topk):
    nqb = seq // Q_BLOCK
    kc = min(KEY_CHUNK, seq)
    kern = functools.partial(_attn_kernel, topk=topk, kc=kc)
    return pl.pallas_call(
        kern,
        grid=(batch, nqb),
        in_specs=[
            pl.BlockSpec((Q_BLOCK, ATTN_W), lambda b, i: (b * nqb + i, 0)),
            pl.BlockSpec((Q_BLOCK, IDXQ_W), lambda b, i: (b * nqb + i, (ATTN_W + 2 * KV_W) // IDXQ_W)),
            pl.BlockSpec((seq, KV_W), lambda b, i: (b, ATTN_W // KV_W)),
            pl.BlockSpec((None, ATTN_KV_HEADS, V_ROWS, seq), lambda b, i: (b, 0, 0, 0)),
            pl.BlockSpec((seq, LANES), lambda b, i: (b, 0)),
            pl.BlockSpec((Q_BLOCK, LANES), lambda b, i: (b * nqb + i, 0)),
        ],
        out_specs=pl.BlockSpec((Q_BLOCK, ATTN_W), lambda b, i: (b * nqb + i, 0)),
        out_shape=jax.ShapeDtypeStruct((batch * seq, ATTN_W), BF16),
        scratch_shapes=[
            pltpu.VMEM((ATTN_HEADS * Q_BLOCK, HEAD_DIM), BF16),
            pltpu.VMEM((IDX_HEADS * Q_BLOCK, IDX_DIM), BF16),
            pltpu.VMEM((seq + kc, LANES), F32),
            pltpu.VMEM((seq, LANES), F32),
            pltpu.VMEM((ATTN_HEADS * V_ROWS, LANES), F32),
            pltpu.VMEM((ATTN_W, LANES), F32),
        ],
        compiler_params=_cparams(2),
        name="dsa_attention",
    )(a_pack, a_pack, a_pack, vt, i_pack, i_pack)


def _retconv_kernel(cb_ref, cc_ref, ch_ref, rq_ref, rk_ref, rv_ref, rg_ref,
                    cw_ref, cos_ref, sin_ref, dec_ref, conv_o, ret_o,
                    halo_ref, state_ref, *, ts):
    j = pl.program_id(1)

    @pl.when(j == 0)
    def _():
        halo_ref[...] = jnp.zeros_like(halo_ref)
        state_ref[...] = jnp.zeros_like(state_ref)

    u = cc_ref[...] * ch_ref[...]
    row = lax.broadcasted_iota(I32, u.shape, 0)
    prev1 = halo_ref[7:8, :]
    prev2 = halo_ref[6:7, :]
    u1 = jnp.where(row == 0, prev1, pltpu.roll(u, 1, 0))
    u2 = jnp.where(row == 0, prev2, jnp.where(row == 1, prev1, pltpu.roll(u, 2, 0)))
    y = cw_ref[0:1, :] * u2 + cw_ref[1:2, :] * u1 + cw_ref[2:3, :] * u
    conv_o[...] = (cb_ref[...] * y).astype(conv_o.dtype)
    halo_ref[...] = u[ts - 8:ts, :]

    cos = cos_ref[...]
    sin = sin_ref[...]
    half = RET_HEAD_DIM // 2
    nt = (((1,), (1,)), ((), ()))
    tn = (((0,), (0,)), ((), ()))
    for h in range(RET_HEADS):
        cols = slice(h * RET_HEAD_DIM, (h + 1) * RET_HEAD_DIM)
        qh = rq_ref[:, cols]
        kh = rk_ref[:, cols]
        qh = qh * cos + pltpu.roll(qh, half, 1) * sin
        kh = (kh * cos + pltpu.roll(kh, half, 1) * sin) * (RET_HEAD_DIM ** -0.5)
        vh = rv_ref[:, cols]
        intra = dec_ref[0, h]
        q_dec = dec_ref[1, h]
        k_dec = dec_ref[2, h]
        c_dec = dec_ref[3, h]
        state = state_ref[h]
        outs = []
        for c in range(ts // RET_CHUNK):
            rows = slice(c * RET_CHUNK, (c + 1) * RET_CHUNK)
            qc, kc_, vc = qh[rows], kh[rows], vh[rows].astype(BF16)
            s = lax.dot_general(qc.astype(BF16), kc_.astype(BF16), nt,
                                preferred_element_type=F32) * intra
            o = (jnp.dot(s.astype(BF16), vc, preferred_element_type=F32)
                 + jnp.dot((qc * q_dec).astype(BF16), state.astype(BF16),
                           preferred_element_type=F32))
            state = state * c_dec + lax.dot_general((kc_ * k_dec).astype(BF16), vc, tn,
                                                    preferred_element_type=F32)
            outs.append(o)
        state_ref[h] = state
        o = jnp.concatenate(outs, axis=0)
        mu = jnp.mean(o, axis=-1, keepdims=True)
        d = o - mu
        var = jnp.mean(d * d, axis=-1, keepdims=True)
        on = d * lax.rsqrt(var + LN_EPS)
        g = rg_ref[:, cols]
        ret_o[:, cols] = (on * (g / (1.0 + jnp.exp(-g)))).astype(ret_o.dtype)


def _retconv(r_pack, conv_w, cos_t, sin_t, dec, batch, seq):
    ts = min(SEQ_TILE, seq)
    nst = seq // ts
    t = batch * seq
    kern = functools.partial(_retconv_kernel, ts=ts)
    specs = [pl.BlockSpec((ts, CONV_DIM),
                          functools.partial(lambda b, j, n: (b * nst + j, n), n=n))
             for n in range(7)]
    return pl.pallas_call(
        kern,
        grid=(batch, nst),
        in_specs=specs + [
            pl.BlockSpec((8, CONV_DIM), lambda b, j: (0, 0)),
            pl.BlockSpec((ts, RET_HEAD_DIM), lambda b, j: (j, 0)),
            pl.BlockSpec((ts, RET_HEAD_DIM), lambda b, j: (j, 0)),
            pl.BlockSpec((4, RET_HEADS, RET_CHUNK, RET_CHUNK), lambda b, j: (0, 0, 0, 0)),
        ],
        out_specs=[pl.BlockSpec((ts, CONV_DIM), lambda b, j: (b * nst + j, 0)),
                   pl.BlockSpec((ts, RET_W), lambda b, j: (b * nst + j, 0))],
        out_shape=[jax.ShapeDtypeStruct((t, CONV_DIM), BF16),
                   jax.ShapeDtypeStruct((t, RET_W), BF16)],
        scratch_shapes=[pltpu.VMEM((8, CONV_DIM), F32),
                        pltpu.VMEM((RET_HEADS, RET_HEAD_DIM, RET_HEAD_DIM), F32)],
        compiler_params=_cparams(2),
        name="conv_retention",
    )(*([r_pack] * 7), conv_w, cos_t, sin_t, dec)


def _layer_norm(y, g, b):
    mu = jnp.mean(y, axis=-1, keepdims=True)
    d = y - mu
    var = jnp.mean(d * d, axis=-1, keepdims=True)
    return d * lax.rsqrt(var + LN_EPS) * g + b


def _outproj_kernel(attn_ref, conv_ref, ret_ref, x_ref, w_ref, g_ref, b_ref, rw_ref, rb_ref,
                    x1_ref, idx_ref, gate_ref, *, alpha):
    mix = jnp.dot(attn_ref[...], w_ref[0:ATTN_W, :], preferred_element_type=F32)
    mix = mix + jnp.dot(conv_ref[...], w_ref[ATTN_W:ATTN_W + CONV_DIM, :],
                        preferred_element_type=F32)
    mix = mix + jnp.dot(ret_ref[...], w_ref[ATTN_W + CONV_DIM:, :], preferred_element_type=F32)
    x1 = _layer_norm(alpha * x_ref[...] + mix, g_ref[...], b_ref[...])
    x1_ref[...] = x1

    nt = (((1,), (1,)), ((), ()))
    logits = lax.dot_general(rw_ref[...], x1.astype(BF16), nt,
                             preferred_element_type=F32) + rb_ref[...]
    eidx = lax.broadcasted_iota(I32, logits.shape, 0)
    vals, idxs = [], []
    for _ in range(TOP_K):
        mx = jnp.max(logits, axis=0, keepdims=True)
        am = jnp.min(jnp.where(logits == mx, eidx, N_EXPERTS), axis=0, keepdims=True)
        vals.append(mx)
        idxs.append(am)
        logits = jnp.where(eidx == am, -jnp.inf, logits)
    v = jnp.concatenate(vals, axis=0)
    e = jnp.exp(v - v[0:1, :])
    gate_ref[...] = e / jnp.sum(e, axis=0, keepdims=True)
    idx_ref[...] = jnp.concatenate(idxs, axis=0)


def _outproj_ln_router(attn, conv, ret, x, w_out, g, b, rw_t, rb, alpha):
    t, d = x.shape
    tm = min(ROW_TILE, t)
    kern = functools.partial(_outproj_kernel, alpha=alpha)
    row = lambda w: pl.BlockSpec((tm, w), lambda i: (i, 0))
    full = lambda s: pl.BlockSpec(s, lambda i: (0,) * len(s))
    return pl.pallas_call(
        kern,
        grid=(t // tm,),
        in_specs=[row(ATTN_W), row(CONV_DIM), row(RET_W), row(d), full((d, d)),
                  full((1, d)), full((1, d)), full((N_EXPERTS, d)), full((N_EXPERTS, 1))],
        out_specs=[row(d), pl.BlockSpec((TOP_K, tm), lambda i: (0, i)),
                   pl.BlockSpec((TOP_K, tm), lambda i: (0, i))],
        out_shape=[jax.ShapeDtypeStruct((t, d), F32),
                   jax.ShapeDtypeStruct((TOP_K, t), I32),
                   jax.ShapeDtypeStruct((TOP_K, t), F32)],
        compiler_params=_cparams(1),
        name="outproj_ln_router",
    )(attn, conv, ret, x, w_out, g, b, rw_t, rb)


def _expert_kernel(te_ref, nu_ref, tokc_ref, tokn_ref, dstp_ref, gate_ref, x_hbm,
                   wg_ref, wu_ref, wd_ref, bg_ref, bu_ref, bd_ref, out_hbm,
                   xbuf, ybuf, xb_ref, act_ref, gsem, ssem, *, tm):
    i = pl.program_id(0)
    n_used = nu_ref[0]
    slot = lax.rem(i, 2)
    other = 1 - slot
    d = xbuf.shape[2]
    n_real = out_hbm.shape[0] - 2 * tm

    def gather_start(tok_ref, r, s):
        pltpu.make_async_copy(x_hbm.at[pl.ds(tok_ref[0, 0, r], 1)],
                              xbuf.at[s, pl.ds(r, 1)], gsem.at[s]).start()

    def scatter_start(dst, r, s):
        pltpu.make_async_copy(ybuf.at[s, pl.ds(r, 1)],
                              out_hbm.at[pl.ds(dst, 1)], ssem.at[s]).start()

    def wait_gather(s):
        pltpu.make_async_copy(x_hbm.at[pl.ds(0, tm)], xbuf.at[s], gsem.at[s]).wait()

    def wait_scatter(s):
        pltpu.make_async_copy(ybuf.at[s], out_hbm.at[pl.ds(0, tm)], ssem.at[s]).wait()

    @pl.when(i == 0)
    def _():
        ybuf[...] = jnp.zeros_like(ybuf)
        for s in range(2):
            pltpu.make_async_copy(ybuf.at[0], out_hbm.at[pl.ds(n_real + s * tm, tm)],
                                  ssem.at[0]).start()
        for s in range(2):
            pltpu.make_async_copy(ybuf.at[0], out_hbm.at[pl.ds(n_real + s * tm, tm)],
                                  ssem.at[0]).wait()

        def body(r, carry):
            gather_start(tokc_ref, r, 0)
            return carry
        lax.fori_loop(0, tm, body, 0, unroll=8)

    @pl.when(i < n_used)
    def _():
        wait_gather(slot)

        @pl.when(i >= 1)
        def _():
            wait_scatter(slot)

        copies = [(True, r) for r in range(tm)] + [(False, r) for r in range(tm)]
        n_seg = 2 * (D_FF // MXU_TILE) + d // MXU_TILE
        per_seg = -(-len(copies) // n_seg)

        def issue(batch):
            rows = [tokn_ref[0, 0, r] if is_gather else dstp_ref[0, 0, r] for is_gather, r in batch]
            for (is_gather, r), row in zip(batch, rows):
                if is_gather:
                    pltpu.make_async_copy(x_hbm.at[pl.ds(row, 1)], xbuf.at[other, pl.ds(r, 1)],
                                          gsem.at[other]).start()
                else:
                    scatter_start(jnp.where(i == 0, n_real + tm + r, row), r, other)

        batches = iter([copies[b:b + per_seg] for b in range(0, len(copies), per_seg)])

        xb_ref[...] = xbuf[slot].astype(BF16)
        for c in range(D_FF // MXU_TILE):
            cols = slice(c * MXU_TILE, (c + 1) * MXU_TILE)
            issue(next(batches, []))
            gt = jnp.dot(xb_ref[...], wg_ref[:, cols], preferred_element_type=F32) + bg_ref[:, cols]
            issue(next(batches, []))
            up = jnp.dot(xb_ref[...], wu_ref[:, cols], preferred_element_type=F32) + bu_ref[:, cols]
            gt = jnp.minimum(gt, SWIGLU_LIMIT)
            up = jnp.clip(up, -SWIGLU_LIMIT, SWIGLU_LIMIT)
            act_ref[:, cols] = ((up + 1.0) * (gt / (1.0 + jnp.exp(-SWIGLU_ALPHA * gt)))).astype(BF16)
        gate = gate_ref[...]
        for c in range(d // MXU_TILE):
            cols = slice(c * MXU_TILE, (c + 1) * MXU_TILE)
            issue(next(batches, []))
            y = jnp.dot(act_ref[...], wd_ref[:, cols], preferred_element_type=F32) + bd_ref[:, cols]
            ybuf[slot, :, cols] = y * gate
        for batch in batches:
            issue(batch)

    @pl.when(i == n_used)
    def _():
        def body(r, carry):
            scatter_start(dstp_ref[0, 0, r], r, other)
            return carry
        lax.fori_loop(0, tm, body, 0, unroll=8)
        wait_scatter(slot)
        wait_scatter(other)
        wait_gather(slot)


def _experts(x1, tile_expert, n_used, row_tok, row_dst, row_gate,
             wg, wu, wd, bg, bu, bd, layer, tm):
    t, d = x1.shape
    n_tiles = tile_expert.shape[0]
    kern = functools.partial(_expert_kernel, tm=tm)
    smem_blk = lambda f: pl.BlockSpec((1, 1, tm), f, memory_space=pltpu.SMEM)
    wspec = lambda s: pl.BlockSpec((None,) + s, lambda i, te, nu: (layer * N_EXPERTS + te[i], 0, 0))
    grid_spec = pltpu.PrefetchScalarGridSpec(
        num_scalar_prefetch=2,
        grid=(n_tiles,),
        in_specs=[
            smem_blk(lambda i, te, nu: (i, 0, 0)),
            smem_blk(lambda i, te, nu: (jnp.minimum(i + 1, n_tiles - 1), 0, 0)),
            smem_blk(lambda i, te, nu: (jnp.maximum(i - 1, 0), 0, 0)),
            pl.BlockSpec((tm, 1), lambda i, te, nu: (i, 0)),
            pl.BlockSpec(memory_space=pl.ANY),
            wspec((d, D_FF)), wspec((d, D_FF)), wspec((D_FF, d)),
            wspec((1, D_FF)), wspec((1, D_FF)), wspec((1, d)),
        ],
        out_specs=pl.BlockSpec(memory_space=pl.ANY),
        scratch_shapes=[pltpu.VMEM((2, tm, d), F32), pltpu.VMEM((2, tm, d), F32),
                        pltpu.VMEM((tm, d), BF16), pltpu.VMEM((tm, D_FF), BF16),
                        pltpu.SemaphoreType.DMA((2,)), pltpu.SemaphoreType.DMA((2,))],
    )
    return pl.pallas_call(
        kern,
        grid_spec=grid_spec,
        out_shape=jax.ShapeDtypeStruct((t * TOP_K + 2 * tm, d), F32),
        compiler_params=_cparams(1),
        name="expert_ffn",
    )(tile_expert, n_used, row_tok, row_tok, row_dst, row_gate, x1,
      wg, wu, wd, bg, bu, bd)


def _route(top_idx, gates, tm):
    t = top_idx.shape[1]
    n_assign = t * TOP_K
    n_tiles = n_assign // tm + N_EXPERTS
    e_flat = top_idx.reshape(-1)
    ids = jnp.arange(n_assign, dtype=I32)
    _, g_sorted, a_sorted = lax.sort((e_flat, gates.reshape(-1), ids), num_keys=1, is_stable=True)
    counts = jnp.sum((e_flat[:, None] == jnp.arange(N_EXPERTS, dtype=I32)[None, :]).astype(I32), axis=0)
    padded = ((counts + tm - 1) // tm) * tm
    start = jnp.cumsum(counts) - counts
    pend = jnp.cumsum(padded)
    pstart = pend - padded
    n_used = (pend[-1] // tm).astype(I32)
    tile = jnp.arange(n_tiles, dtype=I32)
    te = jnp.sum((tile[:, None] * tm >= pend[None, :]).astype(I32), axis=1)
    last = jnp.sum(((n_used - 1) * tm >= pend).astype(I32))
    te = jnp.minimum(te, last)
    in_expert = tile * tm - pstart[te]
    n_valid = jnp.where(tile < n_used, jnp.clip(counts[te] - in_expert, 0, tm), 0)
    first = jnp.where(tile < n_used, start[te] + in_expert, 0)
    window = lambda v: jax.vmap(lambda s: lax.dynamic_slice(v, (s,), (tm,)))(first)
    a_w = window(jnp.pad(a_sorted, (0, tm)))
    g_w = window(jnp.pad(g_sorted, (0, tm)))
    lane = jnp.arange(tm, dtype=I32)[None, :]
    valid = lane < n_valid[:, None]
    row_tok = jnp.where(valid, a_w % t, 0).astype(I32)
    row_dst = jnp.where(valid, a_w, n_assign + (tile[:, None] % 2) * tm + lane).astype(I32)
    row_gate = jnp.where(valid, g_w, 0.0)
    return (te, n_used.reshape(1), row_tok.reshape(n_tiles, 1, tm),
            row_dst.reshape(n_tiles, 1, tm), row_gate.reshape(n_tiles * tm, 1))


def _combine_kernel(y0_ref, y1_ref, y2_ref, y3_ref, x_ref, g_ref, b_ref, o_ref, *, alpha):
    ffn = (y0_ref[...] + y1_ref[...]) + (y2_ref[...] + y3_ref[...])
    o_ref[...] = _layer_norm(alpha * x_ref[...] + ffn, g_ref[...], b_ref[...])


def _combine_ln(y4, x1, g, b, alpha):
    t, d = x1.shape
    tm = min(ROW_TILE, t)
    nb = t // tm
    kern = functools.partial(_combine_kernel, alpha=alpha)
    slot_specs = [pl.BlockSpec((tm, d), functools.partial(lambda i, k: (k * nb + i, 0), k=k))
                  for k in range(TOP_K)]
    return pl.pallas_call(
        kern,
        grid=(nb,),
        in_specs=slot_specs + [pl.BlockSpec((tm, d), lambda i: (i, 0)),
                               pl.BlockSpec((1, d), lambda i: (0, 0)),
                               pl.BlockSpec((1, d), lambda i: (0, 0))],
        out_specs=pl.BlockSpec((tm, d), lambda i: (i, 0)),
        out_shape=jax.ShapeDtypeStruct((t, d), F32),
        compiler_params=_cparams(1),
        name="combine_ln",
    )(y4, y4, y4, y4, x1, g, b)


def _deinterleave_kernel(w_ref, g_ref, u_ref):
    half = MXU_TILE // 2
    src = lax.broadcasted_iota(I32, (MXU_TILE, MXU_TILE), 0)
    dst = lax.broadcasted_iota(I32, (MXU_TILE, MXU_TILE), 1)
    perm = jnp.where(src == jnp.where(dst < half, 2 * dst, 2 * (dst - half) + 1), 1.0, 0.0).astype(BF16)
    for c in range(w_ref.shape[1] // MXU_TILE):
        blk = w_ref[:, c * MXU_TILE:(c + 1) * MXU_TILE].astype(BF16)
        sp = jnp.dot(blk, perm, preferred_element_type=F32)
        g_ref[:, c * half:(c + 1) * half] = sp[:, :half].astype(BF16)
        u_ref[:, c * half:(c + 1) * half] = sp[:, half:].astype(BF16)


def _deinterleave(w_gu):
    n, d, f2 = w_gu.shape
    tr = min(1024, d)
    out = jax.ShapeDtypeStruct((n, d, f2 // 2), BF16)
    return pl.pallas_call(
        _deinterleave_kernel,
        grid=(n, d // tr),
        in_specs=[pl.BlockSpec((None, tr, f2), lambda e, r: (e, r, 0))],
        out_specs=[pl.BlockSpec((None, tr, f2 // 2), lambda e, r: (e, r, 0))] * 2,
        out_shape=[out, out],
        compiler_params=_cparams(2),
        name="split_gate_up",
    )(w_gu)


def _rotary_tables(seq):
    half = RET_HEAD_DIM // 2
    inv_freq = ROPE_BASE ** (-jnp.linspace(0.0, 1.0, half, dtype=F32))
    ang = jnp.arange(seq).astype(F32)[:, None] * inv_freq[None, :]
    cos, sin = jnp.cos(ang), jnp.sin(ang)
    return (jnp.concatenate([cos, cos], axis=-1), jnp.concatenate([-sin, sin], axis=-1))


def _decay_tables():
    c = RET_CHUNK
    lg = jnp.log1p(-jnp.exp2(-5.0 - jnp.arange(RET_HEADS, dtype=F32)))
    i = jnp.arange(c, dtype=F32)
    diff = i[:, None] - i[None, :]
    intra = jnp.where(diff[None] >= 0, jnp.exp(jnp.maximum(diff, 0.0)[None] * lg[:, None, None]), 0.0)
    q_dec = jnp.exp((i + 1.0)[None, :] * lg[:, None])[..., None]
    k_dec = jnp.exp((c - 1.0 - i)[None, :] * lg[:, None])[..., None]
    c_dec = jnp.exp(c * lg)[:, None, None]
    full = (RET_HEADS, c, c)
    return jnp.stack([intra.astype(F32), jnp.broadcast_to(q_dec, full),
                      jnp.broadcast_to(k_dec, full), jnp.broadcast_to(c_dec, full)])


def _split_w_in(w_in):
    pts = np.cumsum((0,) + IN_SIZES)
    a = w_in[..., pts[0]:pts[4]]
    idx = w_in[..., pts[4]:pts[6]]
    idx = jnp.pad(idx, ((0, 0), (0, 0), (0, LANES - idx.shape[-1])))
    rest = w_in[..., pts[6]:]
    return a.astype(BF16), idx.astype(BF16), rest.astype(BF16)


def _v_transposed(a_pack, batch, seq):
    v = a_pack[:, ATTN_W + KV_W:ATTN_W + 2 * KV_W].reshape(batch, seq, ATTN_KV_HEADS, HEAD_DIM)
    vt = jnp.transpose(v, (0, 2, 3, 1))
    ones = jnp.ones((batch, ATTN_KV_HEADS, BF16_ROWS, seq), BF16)
    return jnp.concatenate([vt, ones], axis=2)


def kernel(x, w_in, conv_w, w_out, ln1_g, ln1_b, router_w, router_b,
           w_gu, b_gu, w_down, b_down, ln2_g, ln2_b):
    batch, seq, d = x.shape
    depth = w_in.shape[0]
    t = batch * seq
    alpha = (2 * depth) ** 0.25
    topk = min(IDX_TOPK_MAX, seq // 4)
    tm_e = EXPERT_TILE

    wa, wi, wr = _split_w_in(w_in)
    w_out_b = w_out.astype(BF16)
    rw_t = jnp.swapaxes(router_w, 1, 2).astype(BF16)
    n_exp = depth * N_EXPERTS
    wg, wu = _deinterleave(w_gu.reshape(n_exp, d, 2 * D_FF))
    wd = w_down.reshape(n_exp, D_FF, d).astype(BF16)
    bg = b_gu[..., 0::2].reshape(n_exp, 1, D_FF)
    bu = b_gu[..., 1::2].reshape(n_exp, 1, D_FF)
    bd = b_down.reshape(n_exp, 1, d)
    cw = jnp.pad(conv_w, ((0, 0), (0, 8 - CONV_WIDTH), (0, 0)))
    cos_t, sin_t = _rotary_tables(seq)
    dec = _decay_tables()

    xf = x.reshape(t, d)
    for l in range(depth):
        a_pack = _matmul(xf, wa[l], min(512, t), 1024, BF16, "in_proj_attn")
        i_pack = _matmul(xf, wi[l], min(512, t), LANES, F32, "in_proj_idx")
        r_pack = _matmul(xf, wr[l], min(ROW_TILE, t), wr.shape[-1], F32, "in_proj_rest")
        attn = _attention(a_pack, i_pack, _v_transposed(a_pack, batch, seq), batch, seq, topk)
        conv, ret = _retconv(r_pack, cw[l], cos_t, sin_t, dec, batch, seq)
        x1, top_idx, gates = _outproj_ln_router(
            attn, conv, ret, xf, w_out_b[l], ln1_g[l][None], ln1_b[l][None],
            rw_t[l], router_b[l][:, None], alpha)
        te, n_used, row_tok, row_dst, row_gate = _route(top_idx, gates, tm_e)
        y4 = _experts(x1, te, n_used, row_tok, row_dst, row_gate,
                      wg, wu, wd, bg, bu, bd, l, tm_e)
        xf = _combine_ln(y4, x1, ln2_g[l][None], ln2_b[l][None], alpha)
    return xf.reshape(batch, seq, d)
```

```python
import functools

import jax
import jax.numpy as jnp
import numpy as np
from jax import lax
from jax.experimental import pallas as pl
from jax.experimental.pallas import tpu as pltpu

F32 = jnp.float32
BF16 = jnp.bfloat16
I32 = jnp.int32

ATTN_HEADS = 16
ATTN_KV_HEADS = 4
HEAD_DIM = 64
IDX_HEADS = 8
IDX_DIM = 64
IDX_TOPK_MAX = 256
Q_BLOCK = 128
CONV_DIM = 512
CONV_WIDTH = 3
RET_HEADS = 4
RET_HEAD_DIM = 128
RET_CHUNK = 128
ROPE_BASE = 10000.0
N_EXPERTS = 32
TOP_K = 4
D_FF = 768
SWIGLU_LIMIT = 7.0
SWIGLU_ALPHA = 1.702
LN_EPS = 1e-5

ATTN_W = ATTN_HEADS * HEAD_DIM
KV_W = ATTN_KV_HEADS * HEAD_DIM
IDXQ_W = IDX_HEADS * IDX_DIM
RET_W = RET_HEADS * RET_HEAD_DIM
IN_SIZES = (ATTN_W, KV_W, KV_W, IDXQ_W, IDX_DIM, IDX_HEADS,
            CONV_DIM, CONV_DIM, CONV_DIM, RET_W, RET_W, RET_W, RET_W)

LANES = 128
MXU_TILE = 256
BF16_ROWS = 16
KEY_CHUNK = 256
SEQ_TILE = 512
ROW_TILE = 256
EXPERT_TILE = 256
VMEM_LIMIT = 56 * 1024 * 1024

NEG_BIG = -1e30
V_ROWS = HEAD_DIM + BF16_ROWS
BISECT_STEPS = 20
BISECT_EXTRA = 4
BISECT_MAX_ROUNDS = 80


def _cparams(n_axes):
    return pltpu.CompilerParams(dimension_semantics=("arbitrary",) * n_axes,
                                vmem_limit_bytes=VMEM_LIMIT)


def _mm_kernel(x_ref, w_ref, o_ref):
    o_ref[...] = jnp.dot(x_ref[...].astype(BF16), w_ref[...],
                         preferred_element_type=F32).astype(o_ref.dtype)


def _matmul(x, w, tm, tn, out_dtype, name):
    m, k = x.shape
    n = w.shape[1]
    return pl.pallas_call(
        _mm_kernel,
        grid=(n // tn, m // tm),
        in_specs=[pl.BlockSpec((tm, k), lambda j, i: (i, 0)),
                  pl.BlockSpec((k, tn), lambda j, i: (0, j))],
        out_specs=pl.BlockSpec((tm, tn), lambda j, i: (i, j)),
        out_shape=jax.ShapeDtypeStruct((m, n), out_dtype),
        compiler_params=_cparams(2),
        name=name,
    )(x, w)


def _attn_kernel(q_ref, qi_ref, k_ref, vt_ref, ki_ref, iw_ref, o_ref,
                 qs_ref, qis_ref, s_ref, bias_ref, acc_ref, ot_ref, *, topk, kc):
    i = pl.program_id(1)
    q0 = i * Q_BLOCK
    nch = (q0 + Q_BLOCK + kc - 1) // kc
    kc2 = 2 * kc
    nch2 = (nch + 1) // 2
    nt = (((1,), (1,)), ((), ()))

    scale = HEAD_DIM ** -0.5
    for h in range(ATTN_HEADS):
        qs_ref[h * Q_BLOCK:(h + 1) * Q_BLOCK, :] = (
            q_ref[:, h * HEAD_DIM:(h + 1) * HEAD_DIM] * scale).astype(BF16)
    for h in range(IDX_HEADS):
        qis_ref[h * Q_BLOCK:(h + 1) * Q_BLOCK, :] = qi_ref[:, h * IDX_DIM:(h + 1) * IDX_DIM]

    wt = jnp.transpose(iw_ref[...])[IDX_DIM:IDX_DIM + IDX_HEADS, :] * (IDXQ_W ** -0.5)
    qpos = q0 + lax.broadcasted_iota(I32, (kc, LANES), 1)
    krow = lax.broadcasted_iota(I32, (kc, LANES), 0)

    def score_chunk(c, carry):
        lo, hi = carry
        for half in range(2):
            r0 = pl.multiple_of(c * kc2 + half * kc, kc)
            ki_c = ki_ref[pl.ds(r0, kc), :][:, :IDX_DIM].astype(BF16)
            r = lax.dot_general(ki_c, qis_ref[...], nt, preferred_element_type=F32)
            sc = jnp.zeros((kc, LANES), F32)
            for h in range(IDX_HEADS):
                sc = sc + jnp.maximum(r[:, h * LANES:(h + 1) * LANES], 0.0) * wt[h:h + 1, :]
            causal = r0 + krow <= qpos
            s_ref[pl.ds(r0, kc), :] = jnp.where(causal, sc, -jnp.inf)
            lo = jnp.minimum(lo, jnp.min(jnp.where(causal, sc, jnp.inf), axis=0, keepdims=True))
            hi = jnp.maximum(hi, jnp.max(jnp.where(causal, sc, -jnp.inf), axis=0, keepdims=True))
        return lo, hi

    lo0, hi0 = lax.fori_loop(0, nch2, score_chunk,
                             (jnp.full((1, LANES), jnp.inf, F32), jnp.full((1, LANES), -jnp.inf, F32)))

    def column_sum(vals):
        acc = lax.fori_loop(0, nch2, lambda c, a: a + vals(s_ref[pl.ds(pl.multiple_of(c * kc2, kc2), kc2), :]),
                            jnp.zeros((8, LANES), I32))
        return jnp.sum(acc, axis=0, keepdims=True)

    def count(pred):
        return column_sum(lambda s: jnp.sum(jnp.where(pred(s), 1, 0).reshape(kc2 // 8, 8, LANES), axis=0))

    def bisect(_, carry):
        lo, hi = carry
        mid = lo + (hi - lo) * 0.5
        mid = jnp.where(mid == lo, hi, mid)
        ge = count(lambda s: s >= mid) >= topk
        return jnp.where(ge, mid, lo), jnp.where(ge, hi, mid)

    def settle(lo):
        def body(c, m):
            s = s_ref[pl.ds(pl.multiple_of(c * kc2, kc2), kc2), :]
            return jnp.minimum(m, jnp.min(jnp.where(s >= lo, s, jnp.inf).reshape(kc2 // 8, 8, LANES), axis=0))
        m = lax.fori_loop(0, nch2, body, jnp.full((8, LANES), jnp.inf, F32))
        thr = jnp.min(m, axis=0, keepdims=True)
        return thr, count(lambda s: s > thr)

    lo, hi = lax.fori_loop(0, BISECT_STEPS, bisect, (lo0, hi0))
    thr, cnt_gt = settle(lo)

    def unsettled(state):
        _, _, _, cnt_gt, rounds = state
        return (jnp.max(cnt_gt) >= topk) & (rounds < BISECT_MAX_ROUNDS)

    def refine(state):
        lo, hi, _, _, rounds = state
        lo, hi = lax.fori_loop(0, BISECT_EXTRA, bisect, (lo, hi))
        thr, cnt_gt = settle(lo)
        return lo, hi, thr, cnt_gt, rounds + 1

    _, _, thr, cnt_gt, _ = lax.while_loop(unsettled, refine, (lo, hi, thr, cnt_gt, jnp.int32(0)))
    cnt_eq = count(lambda s: s == thr)
    need = topk - cnt_gt
    has_ties = jnp.max(jnp.where(cnt_eq > need, 1, 0)) > 0

    @pl.when(jnp.logical_not(has_ties))
    def _():
        def body(c, carry):
            r0 = pl.multiple_of(c * kc, kc)
            s = s_ref[pl.ds(r0, kc), :]
            bias_ref[pl.ds(r0, kc), :] = jnp.where(s >= thr, 0.0, NEG_BIG)
            return carry
        lax.fori_loop(0, nch, body, 0)

    @pl.when(has_ties)
    def _():
        tri = (lax.broadcasted_iota(I32, (kc, kc), 1)
               < lax.broadcasted_iota(I32, (kc, kc), 0)).astype(BF16)
        need_f = need.astype(F32)

        def body(c, seen):
            r0 = pl.multiple_of(c * kc, kc)
            s = s_ref[pl.ds(r0, kc), :]
            eq = s == thr
            eq_f = jnp.where(eq, 1.0, 0.0)
            before = jnp.dot(tri, eq_f.astype(BF16), preferred_element_type=F32) + seen
            sel = (s > thr) | (eq & (before < need_f))
            bias_ref[pl.ds(r0, kc), :] = jnp.where(sel, 0.0, NEG_BIG)
            return seen + jnp.sum(eq_f, axis=0, keepdims=True)
        lax.fori_loop(0, nch, body, jnp.zeros((1, LANES), F32))

    acc_ref[...] = jnp.zeros_like(acc_ref)
    rep = ATTN_HEADS // ATTN_KV_HEADS

    def attn_chunk(c, ms):
        r0 = pl.multiple_of(c * kc, kc)
        bias = bias_ref[pl.ds(r0, kc), :]
        k_c = k_ref[pl.ds(r0, kc), :]
        new_ms = []
        sts = []
        for g in range(ATTN_KV_HEADS):
            k_g = k_c[:, g * HEAD_DIM:(g + 1) * HEAD_DIM]
            qs_g = qs_ref[g * rep * Q_BLOCK:(g + 1) * rep * Q_BLOCK, :]
            sts.append(lax.dot_general(k_g, qs_g, nt, preferred_element_type=F32))
        for g in range(ATTN_KV_HEADS):
            vt_g = vt_ref[g, :, pl.ds(r0, kc)]
            for r in range(rep):
                h = g * rep + r
                s = sts[g][:, r * LANES:(r + 1) * LANES] + bias
                m_new = jnp.maximum(ms[h], jnp.max(s, axis=0, keepdims=True))
                alpha = jnp.exp(ms[h] - m_new)
                p = jnp.exp(s - m_new).astype(BF16)
                new_ms.append(m_new)
                pv = jnp.dot(vt_g, p, preferred_element_type=F32)
                rows = slice(h * V_ROWS, (h + 1) * V_ROWS)
                acc_ref[rows, :] = alpha * acc_ref[rows, :] + pv
        return tuple(new_ms)

    init = tuple(jnp.full((1, LANES), NEG_BIG, F32) for _ in range(ATTN_HEADS))
    lax.fori_loop(0, nch, attn_chunk, init)

    for h in range(ATTN_HEADS):
        num = acc_ref[h * V_ROWS:h * V_ROWS + HEAD_DIM, :]
        den = acc_ref[h * V_ROWS + HEAD_DIM:h * V_ROWS + HEAD_DIM + 1, :]
        ot_ref[h * HEAD_DIM:(h + 1) * HEAD_DIM, :] = num / den
    o_ref[...] = jnp.transpose(ot_ref[...]).astype(o_ref.dtype)


def _attention(a_pack, i_pack, vt, batch, seq, topk):
    nqb = seq // Q_BLOCK
    kc = min(KEY_CHUNK, seq)
    kern = functools.partial(_attn_kernel, topk=topk, kc=kc)
    return pl.pallas_call(
        kern,
        grid=(batch, nqb),
        in_specs=[
            pl.BlockSpec((Q_BLOCK, ATTN_W), lambda b, i: (b * nqb + i, 0)),
            pl.BlockSpec((Q_BLOCK, IDXQ_W), lambda b, i: (b * nqb + i, (ATTN_W + 2 * KV_W) // IDXQ_W)),
            pl.BlockSpec((seq, KV_W), lambda b, i: (b, ATTN_W // KV_W)),
            pl.BlockSpec((None, ATTN_KV_HEADS, V_ROWS, seq), lambda b, i: (b, 0, 0, 0)),
            pl.BlockSpec((seq, LANES), lambda b, i: (b, 0)),
            pl.BlockSpec((Q_BLOCK, LANES), lambda b, i: (b * nqb + i, 0)),
        ],
        out_specs=pl.BlockSpec((Q_BLOCK, ATTN_W), lambda b, i: (b * nqb + i, 0)),
        out_shape=jax.ShapeDtypeStruct((batch * seq, ATTN_W), BF16),
        scratch_shapes=[
            pltpu.VMEM((ATTN_HEADS * Q_BLOCK, HEAD_DIM), BF16),
            pltpu.VMEM((IDX_HEADS * Q_BLOCK, IDX_DIM), BF16),
            pltpu.VMEM((seq + kc, LANES), F32),
            pltpu.VMEM((seq, LANES), F32),
            pltpu.VMEM((ATTN_HEADS * V_ROWS, LANES), F32),
            pltpu.VMEM((ATTN_W, LANES), F32),
        ],
        compiler_params=_cparams(2),
        name="dsa_attention",
    )(a_pack, a_pack, a_pack, vt, i_pack, i_pack)


def _retconv_kernel(cb_ref, cc_ref, ch_ref, rq_ref, rk_ref, rv_ref, rg_ref,
                    cw_ref, cos_ref, sin_ref, dec_ref, conv_o, ret_o,
                    halo_ref, state_ref, *, ts):
    j = pl.program_id(1)

    @pl.when(j == 0)
    def _():
        halo_ref[...] = jnp.zeros_like(halo_ref)
        state_ref[...] = jnp.zeros_like(state_ref)

    u = cc_ref[...] * ch_ref[...]
    row = lax.broadcasted_iota(I32, u.shape, 0)
    prev1 = halo_ref[7:8, :]
    prev2 = halo_ref[6:7, :]
    u1 = jnp.where(row == 0, prev1, pltpu.roll(u, 1, 0))
    u2 = jnp.where(row == 0, prev2, jnp.where(row == 1, prev1, pltpu.roll(u, 2, 0)))
    y = cw_ref[0:1, :] * u2 + cw_ref[1:2, :] * u1 + cw_ref[2:3, :] * u
    conv_o[...] = (cb_ref[...] * y).astype(conv_o.dtype)
    halo_ref[...] = u[ts - 8:ts, :]

    cos = cos_ref[...]
    sin = sin_ref[...]
    half = RET_HEAD_DIM // 2
    nt = (((1,), (1,)), ((), ()))
    tn = (((0,), (0,)), ((), ()))
    for h in range(RET_HEADS):
        cols = slice(h * RET_HEAD_DIM, (h + 1) * RET_HEAD_DIM)
        qh = rq_ref[:, cols]
        kh = rk_ref[:, cols]
        qh = qh * cos + pltpu.roll(qh, half, 1) * sin
        kh = (kh * cos + pltpu.roll(kh, half, 1) * sin) * (RET_HEAD_DIM ** -0.5)
        vh = rv_ref[:, cols]
        intra = dec_ref[0, h]
        q_dec = dec_ref[1, h]
        k_dec = dec_ref[2, h]
        c_dec = dec_ref[3, h]
        state = state_ref[h]
        outs = []
        for c in range(ts // RET_CHUNK):
            rows = slice(c * RET_CHUNK, (c + 1) * RET_CHUNK)
            qc, kc_, vc = qh[rows], kh[rows], vh[rows].astype(BF16)
            s = lax.dot_general(qc.astype(BF16), kc_.astype(BF16), nt,
                                preferred_element_type=F32) * intra
            o = (jnp.dot(s.astype(BF16), vc, preferred_element_type=F32)
                 + jnp.dot((qc * q_dec).astype(BF16), state.astype(BF16),
                           preferred_element_type=F32))
            state = state * c_dec + lax.dot_general((kc_ * k_dec).astype(BF16), vc, tn,
                                                    preferred_element_type=F32)
            outs.append(o)
        state_ref[h] = state
        o = jnp.concatenate(outs, axis=0)
        mu = jnp.mean(o, axis=-1, keepdims=True)
        d = o - mu
        var = jnp.mean(d * d, axis=-1, keepdims=True)
        on = d * lax.rsqrt(var + LN_EPS)
        g = rg_ref[:, cols]
        ret_o[:, cols] = (on * (g / (1.0 + jnp.exp(-g)))).astype(ret_o.dtype)


def _retconv(r_pack, conv_w, cos_t, sin_t, dec, batch, seq):
    ts = min(SEQ_TILE, seq)
    nst = seq // ts
    t = batch * seq
    kern = functools.partial(_retconv_kernel, ts=ts)
    specs = [pl.BlockSpec((ts, CONV_DIM),
                          functools.partial(lambda b, j, n: (b * nst + j, n), n=n))
             for n in range(7)]
    return pl.pallas_call(
        kern,
        grid=(batch, nst),
        in_specs=specs + [
            pl.BlockSpec((8, CONV_DIM), lambda b, j: (0, 0)),
            pl.BlockSpec((ts, RET_HEAD_DIM), lambda b, j: (j, 0)),
            pl.BlockSpec((ts, RET_HEAD_DIM), lambda b, j: (j, 0)),
            pl.BlockSpec((4, RET_HEADS, RET_CHUNK, RET_CHUNK), lambda b, j: (0, 0, 0, 0)),
        ],
        out_specs=[pl.BlockSpec((ts, CONV_DIM), lambda b, j: (b * nst + j, 0)),
                   pl.BlockSpec((ts, RET_W), lambda b, j: (b * nst + j, 0))],
        out_shape=[jax.ShapeDtypeStruct((t, CONV_DIM), BF16),
                   jax.ShapeDtypeStruct((t, RET_W), BF16)],
        scratch_shapes=[pltpu.VMEM((8, CONV_DIM), F32),
                        pltpu.VMEM((RET_HEADS, RET_HEAD_DIM, RET_HEAD_DIM), F32)],
        compiler_params=_cparams(2),
        name="conv_retention",
    )(*([r_pack] * 7), conv_w, cos_t, sin_t, dec)


def _layer_norm(y, g, b):
    mu = jnp.mean(y, axis=-1, keepdims=True)
    d = y - mu
    var = jnp.mean(d * d, axis=-1, keepdims=True)
    return d * lax.rsqrt(var + LN_EPS) * g + b


def _outproj_kernel(attn_ref, conv_ref, ret_ref, x_ref, w_ref, g_ref, b_ref, rw_ref, rb_ref,
                    x1_ref, idx_ref, gate_ref, *, alpha):
    mix = jnp.dot(attn_ref[...], w_ref[0:ATTN_W, :], preferred_element_type=F32)
    mix = mix + jnp.dot(conv_ref[...], w_ref[ATTN_W:ATTN_W + CONV_DIM, :],
                        preferred_element_type=F32)
    mix = mix + jnp.dot(ret_ref[...], w_ref[ATTN_W + CONV_DIM:, :], preferred_element_type=F32)
    x1 = _layer_norm(alpha * x_ref[...] + mix, g_ref[...], b_ref[...])
    x1_ref[...] = x1

    nt = (((1,), (1,)), ((), ()))
    logits = lax.dot_general(rw_ref[...], x1.astype(BF16), nt,
                             preferred_element_type=F32) + rb_ref[...]
    eidx = lax.broadcasted_iota(I32, logits.shape, 0)
    vals, idxs = [], []
    for _ in range(TOP_K):
        mx = jnp.max(logits, axis=0, keepdims=True)
        am = jnp.min(jnp.where(logits == mx, eidx, N_EXPERTS), axis=0, keepdims=True)
        vals.append(mx)
        idxs.append(am)
        logits = jnp.where(eidx == am, -jnp.inf, logits)
    v = jnp.concatenate(vals, axis=0)
    e = jnp.exp(v - v[0:1, :])
    gates = e / jnp.sum(e, axis=0, keepdims=True)
    gate_ref[...] = jnp.concatenate([gates, jnp.zeros_like(gates)], axis=0)
    idx_ref[...] = jnp.concatenate(idxs, axis=0)


def _outproj_ln_router(attn, conv, ret, x, w_out, g, b, rw_t, rb, alpha):
    t, d = x.shape
    tm = min(ROW_TILE, t)
    kern = functools.partial(_outproj_kernel, alpha=alpha)
    row = lambda w: pl.BlockSpec((tm, w), lambda i: (i, 0))
    full = lambda s: pl.BlockSpec(s, lambda i: (0,) * len(s))
    return pl.pallas_call(
        kern,
        grid=(t // tm,),
        in_specs=[row(ATTN_W), row(CONV_DIM), row(RET_W), row(d), full((d, d)),
                  full((1, d)), full((1, d)), full((N_EXPERTS, d)), full((N_EXPERTS, 1))],
        out_specs=[row(d),
                   pl.BlockSpec((TOP_K, tm), lambda i: (0, i)),
                   pl.BlockSpec((2 * TOP_K, tm), lambda i: (0, i))],
        out_shape=[jax.ShapeDtypeStruct((t, d), F32),
                   jax.ShapeDtypeStruct((TOP_K, t), I32),
                   jax.ShapeDtypeStruct((2 * TOP_K, t), F32)],
        compiler_params=_cparams(1),
        name="outproj_ln_router",
    )(attn, conv, ret, x, w_out, g, b, rw_t, rb)


def _expert_kernel(te_ref, nu_ref, tokc_ref, tokn_ref, dst_ref, x_hbm,
                   wg_ref, wu_ref, wd_ref, bg_ref, bu_ref, bd_ref, out_hbm,
                   xbuf, ybuf, xb_ref, act_ref, gsem, ssem, *, tm):
    i = pl.program_id(0)
    n_used = nu_ref[0]
    slot = lax.rem(i, 2)
    d = xbuf.shape[2]
    n_real = out_hbm.shape[0] - 2 * tm

    def start_gather(tok_ref, s):
        def body(r, carry):
            pltpu.make_async_copy(x_hbm.at[pl.ds(tok_ref[0, 0, r], 1)],
                                  xbuf.at[s, pl.ds(r, 1)], gsem.at[s]).start()
            return carry
        lax.fori_loop(0, tm, body, 0, unroll=8)

    def wait_gather(s):
        pltpu.make_async_copy(x_hbm.at[pl.ds(0, tm)], xbuf.at[s], gsem.at[s]).wait()

    def wait_scatter(s):
        pltpu.make_async_copy(ybuf.at[s], out_hbm.at[pl.ds(0, tm)], ssem.at[s]).wait()

    @pl.when(i == 0)
    def _():
        ybuf[0] = jnp.zeros(ybuf.shape[1:], F32)
        for s in range(2):
            pltpu.make_async_copy(ybuf.at[0], out_hbm.at[pl.ds(n_real + s * tm, tm)],
                                  ssem.at[0]).start()
        for s in range(2):
            pltpu.make_async_copy(ybuf.at[0], out_hbm.at[pl.ds(n_real + s * tm, tm)],
                                  ssem.at[0]).wait()

    @pl.when((i == 0) & (n_used > 0))
    def _():
        start_gather(tokc_ref, 0)

    @pl.when(i + 1 < n_used)
    def _():
        start_gather(tokn_ref, 1 - slot)

    @pl.when(i < n_used)
    def _():
        wait_gather(slot)
        xb_ref[...] = xbuf[slot].astype(BF16)
        for c in range(D_FF // MXU_TILE):
            cols = slice(c * MXU_TILE, (c + 1) * MXU_TILE)
            gt = jnp.dot(xb_ref[...], wg_ref[:, cols], preferred_element_type=F32) + bg_ref[:, cols]
            up = jnp.dot(xb_ref[...], wu_ref[:, cols], preferred_element_type=F32) + bu_ref[:, cols]
            gt = jnp.minimum(gt, SWIGLU_LIMIT)
            up = jnp.clip(up, -SWIGLU_LIMIT, SWIGLU_LIMIT)
            act_ref[:, cols] = ((up + 1.0) * (gt / (1.0 + jnp.exp(-SWIGLU_ALPHA * gt)))).astype(BF16)
        for c in range(d // MXU_TILE):
            cols = slice(c * MXU_TILE, (c + 1) * MXU_TILE)
            ybuf[slot, :, cols] = (jnp.dot(act_ref[...], wd_ref[:, cols], preferred_element_type=F32)
                                   + bd_ref[:, cols])

        def body(r, carry):
            pltpu.make_async_copy(ybuf.at[slot, pl.ds(r, 1)],
                                  out_hbm.at[pl.ds(dst_ref[0, 0, r], 1)], ssem.at[slot]).start()
            return carry
        lax.fori_loop(0, tm, body, 0, unroll=8)

        @pl.when(i >= 1)
        def _():
            wait_scatter(1 - slot)

        @pl.when(i == n_used - 1)
        def _():
            wait_scatter(slot)


def _experts(x1, tile_expert, n_used, row_tok, row_dst, wg, wu, wd, bg, bu, bd, layer, tm):
    t, d = x1.shape
    n_tiles = tile_expert.shape[0]
    kern = functools.partial(_expert_kernel, tm=tm)
    smem_blk = lambda f: pl.BlockSpec((1, 1, tm), f, memory_space=pltpu.SMEM)
    wspec = lambda s: pl.BlockSpec((None,) + s, lambda i, te, nu: (layer * N_EXPERTS + te[i], 0, 0))
    grid_spec = pltpu.PrefetchScalarGridSpec(
        num_scalar_prefetch=2,
        grid=(n_tiles,),
        in_specs=[
            smem_blk(lambda i, te, nu: (i, 0, 0)),
            smem_blk(lambda i, te, nu: (jnp.minimum(i + 1, n_tiles - 1), 0, 0)),
            smem_blk(lambda i, te, nu: (i, 0, 0)),
            pl.BlockSpec(memory_space=pl.ANY),
            wspec((d, D_FF)), wspec((d, D_FF)), wspec((D_FF, d)),
            wspec((1, D_FF)), wspec((1, D_FF)), wspec((1, d)),
        ],
        out_specs=pl.BlockSpec(memory_space=pl.ANY),
        scratch_shapes=[pltpu.VMEM((2, tm, d), F32), pltpu.VMEM((2, tm, d), F32),
                        pltpu.VMEM((tm, d), BF16), pltpu.VMEM((tm, D_FF), BF16),
                        pltpu.SemaphoreType.DMA((2,)), pltpu.SemaphoreType.DMA((2,))],
    )
    return pl.pallas_call(
        kern,
        grid_spec=grid_spec,
        out_shape=jax.ShapeDtypeStruct((t * TOP_K + 2 * tm, d), F32),
        compiler_params=_cparams(1),
        name="expert_ffn",
    )(tile_expert, n_used, row_tok, row_tok, row_dst, x1, wg, wu, wd, bg, bu, bd)


def _row_table_kernel(first_ref, nvalid_ref, a_ref, tok_ref, dst_ref, *, tm, t, n_tiles):
    n_assign = TOP_K * t
    rows_per_tile = tm // LANES
    lane = lax.broadcasted_iota(I32, (rows_per_tile, LANES), 1)
    pos = lax.broadcasted_iota(I32, (rows_per_tile, LANES), 0) * LANES + lane

    def body(i, carry):
        first = first_ref[i]
        q = lax.shift_right_logical(first, 7)
        m = first & (LANES - 1)
        rows = a_ref[pl.ds(q, rows_per_tile + 1), :]
        rolled = pltpu.roll(rows, lax.rem(LANES - m, LANES), 1)
        a = jnp.where(lane + m < LANES, rolled[:rows_per_tile], rolled[1:])
        valid = pos < nvalid_ref[i]
        tok = a
        for k in range(1, TOP_K):
            tok = tok - jnp.where(a >= k * t, t, 0)
        out = pl.ds(i * rows_per_tile, rows_per_tile)
        tok_ref[out, :] = jnp.where(valid, tok, 0)
        dst_ref[out, :] = jnp.where(valid, a, n_assign + lax.rem(i, 2) * tm + pos)
        return carry

    lax.fori_loop(0, n_tiles, body, 0)


def _route(top_idx, tm):
    t = top_idx.shape[1]
    n_assign = t * TOP_K
    n_tiles = n_assign // tm + N_EXPERTS
    e_flat = top_idx.reshape(-1)
    ids = jnp.arange(n_assign, dtype=I32)
    _, a_sorted = lax.sort((e_flat, ids), num_keys=1, is_stable=True)
    counts = jnp.sum((e_flat[:, None] == jnp.arange(N_EXPERTS, dtype=I32)[None, :]).astype(I32), axis=0)
    padded = ((counts + tm - 1) // tm) * tm
    start = jnp.cumsum(counts) - counts
    pend = jnp.cumsum(padded)
    pstart = pend - padded
    n_used = (pend[-1] // tm).astype(I32)
    tile = jnp.arange(n_tiles, dtype=I32)
    te = jnp.sum((tile[:, None] * tm >= pend[None, :]).astype(I32), axis=1)
    last = jnp.sum(((n_used - 1) * tm >= pend).astype(I32))
    te = jnp.minimum(te, last)
    in_expert = tile * tm - pstart[te]
    n_valid = jnp.where(tile < n_used, jnp.clip(counts[te] - in_expert, 0, tm), 0).astype(I32)
    first = jnp.where(tile < n_used, start[te] + in_expert, 0).astype(I32)

    a2 = jnp.pad(a_sorted, (0, 8 * LANES)).reshape(-1, LANES)
    rows_out = n_tiles * tm // LANES
    kern = functools.partial(_row_table_kernel, tm=tm, t=t, n_tiles=n_tiles)
    table = jax.ShapeDtypeStruct((rows_out, LANES), I32)
    row_tok, row_dst = pl.pallas_call(
        kern,
        grid_spec=pltpu.PrefetchScalarGridSpec(
            num_scalar_prefetch=2,
            grid=(1,),
            in_specs=[pl.BlockSpec(a2.shape, lambda i, f, nv: (0, 0))],
            out_specs=[pl.BlockSpec((rows_out, LANES), lambda i, f, nv: (0, 0))] * 2,
        ),
        out_shape=[table, table],
        compiler_params=_cparams(1),
        name="row_tables",
    )(first, n_valid, a2)
    return (te, n_used.reshape(1), row_tok.reshape(n_tiles, 1, tm), row_dst.reshape(n_tiles, 1, tm))


def _combine_kernel(y0_ref, y1_ref, y2_ref, y3_ref, x_ref, gate_ref, g_ref, b_ref, o_ref, *, alpha):
    tm = x_ref.shape[0]
    cols = []
    for p in range(tm // LANES):
        piece = jnp.concatenate([gate_ref[:, p * LANES:(p + 1) * LANES],
                                 jnp.zeros((LANES - gate_ref.shape[0], LANES), F32)], axis=0)
        cols.append(jnp.transpose(piece))
    gates = [jnp.concatenate([c[:, k:k + 1] for c in cols], axis=0) for k in range(TOP_K)]
    ffn = ((gates[0] * y0_ref[...] + gates[1] * y1_ref[...])
           + (gates[2] * y2_ref[...] + gates[3] * y3_ref[...]))
    o_ref[...] = _layer_norm(alpha * x_ref[...] + ffn, g_ref[...], b_ref[...])


def _combine_ln(y4, x1, gates, g, b, alpha):
    t, d = x1.shape
    tm = min(ROW_TILE, t)
    nb = t // tm
    kern = functools.partial(_combine_kernel, alpha=alpha)
    slot_specs = [pl.BlockSpec((tm, d), functools.partial(lambda i, k: (k * nb + i, 0), k=k))
                  for k in range(TOP_K)]
    return pl.pallas_call(
        kern,
        grid=(nb,),
        in_specs=slot_specs + [pl.BlockSpec((tm, d), lambda i: (i, 0)),
                               pl.BlockSpec((gates.shape[0], tm), lambda i: (0, i)),
                               pl.BlockSpec((1, d), lambda i: (0, 0)),
                               pl.BlockSpec((1, d), lambda i: (0, 0))],
        out_specs=pl.BlockSpec((tm, d), lambda i: (i, 0)),
        out_shape=jax.ShapeDtypeStruct((t, d), F32),
        compiler_params=_cparams(1),
        name="combine_ln",
    )(y4, y4, y4, y4, x1, gates, g, b)


def _deinterleave_kernel(w_ref, g_ref, u_ref):
    half = MXU_TILE // 2
    src = lax.broadcasted_iota(I32, (MXU_TILE, MXU_TILE), 0)
    dst = lax.broadcasted_iota(I32, (MXU_TILE, MXU_TILE), 1)
    perm = jnp.where(src == jnp.where(dst < half, 2 * dst, 2 * (dst - half) + 1), 1.0, 0.0).astype(BF16)
    for c in range(w_ref.shape[1] // MXU_TILE):
        blk = w_ref[:, c * MXU_TILE:(c + 1) * MXU_TILE].astype(BF16)
        sp = jnp.dot(blk, perm, preferred_element_type=F32)
        g_ref[:, c * half:(c + 1) * half] = sp[:, :half].astype(BF16)
        u_ref[:, c * half:(c + 1) * half] = sp[:, half:].astype(BF16)


def _deinterleave(w_gu):
    n, d, f2 = w_gu.shape
    tr = min(1024, d)
    out = jax.ShapeDtypeStruct((n, d, f2 // 2), BF16)
    return pl.pallas_call(
        _deinterleave_kernel,
        grid=(n, d // tr),
        in_specs=[pl.BlockSpec((None, tr, f2), lambda e, r: (e, r, 0))],
        out_specs=[pl.BlockSpec((None, tr, f2 // 2), lambda e, r: (e, r, 0))] * 2,
        out_shape=[out, out],
        compiler_params=_cparams(2),
        name="split_gate_up",
    )(w_gu)


def _rotary_tables(seq):
    half = RET_HEAD_DIM // 2
    inv_freq = ROPE_BASE ** (-jnp.linspace(0.0, 1.0, half, dtype=F32))
    ang = jnp.arange(seq).astype(F32)[:, None] * inv_freq[None, :]
    cos, sin = jnp.cos(ang), jnp.sin(ang)
    return (jnp.concatenate([cos, cos], axis=-1), jnp.concatenate([-sin, sin], axis=-1))


def _decay_tables():
    c = RET_CHUNK
    lg = jnp.log1p(-jnp.exp2(-5.0 - jnp.arange(RET_HEADS, dtype=F32)))
    i = jnp.arange(c, dtype=F32)
    diff = i[:, None] - i[None, :]
    intra = jnp.where(diff[None] >= 0, jnp.exp(jnp.maximum(diff, 0.0)[None] * lg[:, None, None]), 0.0)
    q_dec = jnp.exp((i + 1.0)[None, :] * lg[:, None])[..., None]
    k_dec = jnp.exp((c - 1.0 - i)[None, :] * lg[:, None])[..., None]
    c_dec = jnp.exp(c * lg)[:, None, None]
    full = (RET_HEADS, c, c)
    return jnp.stack([intra.astype(F32), jnp.broadcast_to(q_dec, full),
                      jnp.broadcast_to(k_dec, full), jnp.broadcast_to(c_dec, full)])


def _split_w_in(w_in):
    pts = np.cumsum((0,) + IN_SIZES)
    a = w_in[..., pts[0]:pts[4]]
    idx = w_in[..., pts[4]:pts[6]]
    idx = jnp.pad(idx, ((0, 0), (0, 0), (0, LANES - idx.shape[-1])))
    rest = w_in[..., pts[6]:]
    return a.astype(BF16), idx.astype(BF16), rest.astype(BF16)


def _v_transposed(a_pack, batch, seq):
    v = a_pack[:, ATTN_W + KV_W:ATTN_W + 2 * KV_W].reshape(batch, seq, ATTN_KV_HEADS, HEAD_DIM)
    vt = jnp.transpose(v, (0, 2, 3, 1))
    ones = jnp.ones((batch, ATTN_KV_HEADS, BF16_ROWS, seq), BF16)
    return jnp.concatenate([vt, ones], axis=2)


def kernel(x, w_in, conv_w, w_out, ln1_g, ln1_b, router_w, router_b,
           w_gu, b_gu, w_down, b_down, ln2_g, ln2_b):
    batch, seq, d = x.shape
    depth = w_in.shape[0]
    t = batch * seq
    alpha = (2 * depth) ** 0.25
    topk = min(IDX_TOPK_MAX, seq // 4)
    tm_e = EXPERT_TILE

    wa, wi, wr = _split_w_in(w_in)
    w_out_b = w_out.astype(BF16)
    rw_t = jnp.swapaxes(router_w, 1, 2).astype(BF16)
    n_exp = depth * N_EXPERTS
    wg, wu = _deinterleave(w_gu.reshape(n_exp, d, 2 * D_FF))
    wd = w_down.reshape(n_exp, D_FF, d).astype(BF16)
    bg = b_gu[..., 0::2].reshape(n_exp, 1, D_FF)
    bu = b_gu[..., 1::2].reshape(n_exp, 1, D_FF)
    bd = b_down.reshape(n_exp, 1, d)
    cw = jnp.pad(conv_w, ((0, 0), (0, 8 - CONV_WIDTH), (0, 0)))
    cos_t, sin_t = _rotary_tables(seq)
    dec = _decay_tables()

    xf = x.reshape(t, d)
    for l in range(depth):
        a_pack = _matmul(xf, wa[l], min(512, t), 1024, BF16, "in_proj_attn")
        i_pack = _matmul(xf, wi[l], min(512, t), LANES, F32, "in_proj_idx")
        r_pack = _matmul(xf, wr[l], min(ROW_TILE, t), wr.shape[-1], F32, "in_proj_rest")
        attn = _attention(a_pack, i_pack, _v_transposed(a_pack, batch, seq), batch, seq, topk)
        conv, ret = _retconv(r_pack, cw[l], cos_t, sin_t, dec, batch, seq)
        x1, top_idx, gates = _outproj_ln_router(
            attn, conv, ret, xf, w_out_b[l], ln1_g[l][None], ln1_b[l][None],
            rw_t[l], router_b[l][:, None], alpha)
        te, n_used, row_tok, row_dst = _route(top_idx, tm_e)
        y4 = _experts(x1, te, n_used, row_tok, row_dst, wg, wu, wd, bg, bu, bd, l, tm_e)
        xf = _combine_ln(y4, x1, gates, ln2_g[l][None], ln2_b[l][None], alpha)
    return xf.reshape(batch, seq, d)
```

```python
import functools

import jax
import jax.numpy as jnp
import numpy as np
from jax import lax
from jax.experimental import pallas as pl
from jax.experimental.pallas import tpu as pltpu

F32 = jnp.float32
BF16 = jnp.bfloat16
I32 = jnp.int32

ATTN_HEADS = 16
ATTN_KV_HEADS = 4
HEAD_DIM = 64
IDX_HEADS = 8
IDX_DIM = 64
IDX_TOPK_MAX = 256
Q_BLOCK = 128
CONV_DIM = 512
CONV_WIDTH = 3
RET_HEADS = 4
RET_HEAD_DIM = 128
RET_CHUNK = 128
ROPE_BASE = 10000.0
N_EXPERTS = 32
TOP_K = 4
D_FF = 768
SWIGLU_LIMIT = 7.0
SWIGLU_ALPHA = 1.702
LN_EPS = 1e-5

ATTN_W = ATTN_HEADS * HEAD_DIM
KV_W = ATTN_KV_HEADS * HEAD_DIM
IDXQ_W = IDX_HEADS * IDX_DIM
RET_W = RET_HEADS * RET_HEAD_DIM
IN_SIZES = (ATTN_W, KV_W, KV_W, IDXQ_W, IDX_DIM, IDX_HEADS,
            CONV_DIM, CONV_DIM, CONV_DIM, RET_W, RET_W, RET_W, RET_W)

LANES = 128
MXU_TILE = 256
BF16_ROWS = 16
KEY_CHUNK = 512
SEQ_TILE = 512
ROW_TILE = 256
EXPERT_TILE = 256
VMEM_LIMIT = 56 * 1024 * 1024

NEG_BIG = -1e30
V_ROWS = HEAD_DIM + BF16_ROWS
BISECT_STEPS = 20
BISECT_EXTRA = 4
BISECT_MAX_ROUNDS = 80


def _cparams(n_axes):
    return pltpu.CompilerParams(dimension_semantics=("arbitrary",) * n_axes,
                                vmem_limit_bytes=VMEM_LIMIT)


def _mm_kernel(x_ref, w_ref, o_ref):
    o_ref[...] = jnp.dot(x_ref[...].astype(BF16), w_ref[...],
                         preferred_element_type=F32).astype(o_ref.dtype)


def _matmul(x, w, tm, tn, out_dtype, name):
    m, k = x.shape
    n = w.shape[1]
    return pl.pallas_call(
        _mm_kernel,
        grid=(n // tn, m // tm),
        in_specs=[pl.BlockSpec((tm, k), lambda j, i: (i, 0)),
                  pl.BlockSpec((k, tn), lambda j, i: (0, j))],
        out_specs=pl.BlockSpec((tm, tn), lambda j, i: (i, j)),
        out_shape=jax.ShapeDtypeStruct((m, n), out_dtype),
        compiler_params=_cparams(2),
        name=name,
    )(x, w)


def _attn_kernel(q_ref, qi_ref, k_ref, vt_ref, ki_ref, iw_ref, o_ref,
                 qs_ref, qis_ref, s_ref, bias_ref, acc_ref, ot_ref, *, topk, kc, halves):
    i = pl.program_id(1)
    q0 = i * Q_BLOCK
    nch = (q0 + Q_BLOCK + kc - 1) // kc
    kc2 = halves * kc
    nch2 = (nch + halves - 1) // halves
    nt = (((1,), (1,)), ((), ()))

    scale = HEAD_DIM ** -0.5
    for h in range(ATTN_HEADS):
        qs_ref[h * Q_BLOCK:(h + 1) * Q_BLOCK, :] = (
            q_ref[:, h * HEAD_DIM:(h + 1) * HEAD_DIM] * scale).astype(BF16)
    for h in range(IDX_HEADS):
        qis_ref[h * Q_BLOCK:(h + 1) * Q_BLOCK, :] = qi_ref[:, h * IDX_DIM:(h + 1) * IDX_DIM]

    wt = jnp.transpose(iw_ref[...])[IDX_DIM:IDX_DIM + IDX_HEADS, :] * (IDXQ_W ** -0.5)
    qpos = q0 + lax.broadcasted_iota(I32, (kc, LANES), 1)
    krow = lax.broadcasted_iota(I32, (kc, LANES), 0)

    def score_chunk(c, carry):
        lo, hi = carry
        for half in range(halves):
            r0 = pl.multiple_of(c * kc2 + half * kc, kc)
            ki_c = ki_ref[pl.ds(r0, kc), :][:, :IDX_DIM].astype(BF16)
            r = lax.dot_general(ki_c, qis_ref[...], nt, preferred_element_type=F32)
            sc = jnp.zeros((kc, LANES), F32)
            for h in range(IDX_HEADS):
                sc = sc + jnp.maximum(r[:, h * LANES:(h + 1) * LANES], 0.0) * wt[h:h + 1, :]
            causal = r0 + krow <= qpos
            s_ref[pl.ds(r0, kc), :] = jnp.where(causal, sc, -jnp.inf)
            lo = jnp.minimum(lo, jnp.min(jnp.where(causal, sc, jnp.inf), axis=0, keepdims=True))
            hi = jnp.maximum(hi, jnp.max(jnp.where(causal, sc, -jnp.inf), axis=0, keepdims=True))
        return lo, hi

    lo0, hi0 = lax.fori_loop(0, nch2, score_chunk,
                             (jnp.full((1, LANES), jnp.inf, F32), jnp.full((1, LANES), -jnp.inf, F32)))

    def column_sum(vals):
        acc = lax.fori_loop(0, nch, lambda c, a: a + vals(s_ref[pl.ds(pl.multiple_of(c * kc, kc), kc), :]),
                            jnp.zeros((8, LANES), I32))
        return jnp.sum(acc, axis=0, keepdims=True)

    def count(pred):
        return column_sum(lambda s: jnp.sum(jnp.where(pred(s), 1, 0).reshape(kc // 8, 8, LANES), axis=0))

    def bisect(_, carry):
        lo, hi = carry
        mid = lo + (hi - lo) * 0.5
        mid = jnp.where(mid == lo, hi, mid)
        ge = count(lambda s: s >= mid) >= topk
        return jnp.where(ge, mid, lo), jnp.where(ge, hi, mid)

    def settle(lo):
        def body(c, m):
            s = s_ref[pl.ds(pl.multiple_of(c * kc, kc), kc), :]
            return jnp.minimum(m, jnp.min(jnp.where(s >= lo, s, jnp.inf).reshape(kc // 8, 8, LANES), axis=0))
        m = lax.fori_loop(0, nch, body, jnp.full((8, LANES), jnp.inf, F32))
        thr = jnp.min(m, axis=0, keepdims=True)
        return thr, count(lambda s: s > thr)

    lo, hi = lax.fori_loop(0, BISECT_STEPS, bisect, (lo0, hi0))
    thr, cnt_gt = settle(lo)

    def unsettled(state):
        _, _, _, cnt_gt, rounds = state
        return (jnp.max(cnt_gt) >= topk) & (rounds < BISECT_MAX_ROUNDS)

    def refine(state):
        lo, hi, _, _, rounds = state
        lo, hi = lax.fori_loop(0, BISECT_EXTRA, bisect, (lo, hi))
        thr, cnt_gt = settle(lo)
        return lo, hi, thr, cnt_gt, rounds + 1

    _, _, thr, cnt_gt, _ = lax.while_loop(unsettled, refine, (lo, hi, thr, cnt_gt, jnp.int32(0)))
    cnt_eq = count(lambda s: s == thr)
    need = topk - cnt_gt
    has_ties = jnp.max(jnp.where(cnt_eq > need, 1, 0)) > 0

    @pl.when(jnp.logical_not(has_ties))
    def _():
        def body(c, carry):
            r0 = pl.multiple_of(c * kc, kc)
            s = s_ref[pl.ds(r0, kc), :]
            bias_ref[pl.ds(r0, kc), :] = jnp.where(s >= thr, 0.0, NEG_BIG)
            return carry
        lax.fori_loop(0, nch, body, 0)

    @pl.when(has_ties)
    def _():
        tri = (lax.broadcasted_iota(I32, (kc, kc), 1)
               < lax.broadcasted_iota(I32, (kc, kc), 0)).astype(BF16)
        need_f = need.astype(F32)

        def body(c, seen):
            r0 = pl.multiple_of(c * kc, kc)
            s = s_ref[pl.ds(r0, kc), :]
            eq = s == thr
            eq_f = jnp.where(eq, 1.0, 0.0)
            before = jnp.dot(tri, eq_f.astype(BF16), preferred_element_type=F32) + seen
            sel = (s > thr) | (eq & (before < need_f))
            bias_ref[pl.ds(r0, kc), :] = jnp.where(sel, 0.0, NEG_BIG)
            return seen + jnp.sum(eq_f, axis=0, keepdims=True)
        lax.fori_loop(0, nch, body, jnp.zeros((1, LANES), F32))

    acc_ref[...] = jnp.zeros_like(acc_ref)
    rep = ATTN_HEADS // ATTN_KV_HEADS

    def attn_chunk(c, ms):
        r0 = pl.multiple_of(c * kc, kc)
        bias = bias_ref[pl.ds(r0, kc), :]
        k_c = k_ref[pl.ds(r0, kc), :]
        new_ms = []
        sts = []
        for g in range(ATTN_KV_HEADS):
            k_g = k_c[:, g * HEAD_DIM:(g + 1) * HEAD_DIM]
            qs_g = qs_ref[g * rep * Q_BLOCK:(g + 1) * rep * Q_BLOCK, :]
            sts.append(lax.dot_general(k_g, qs_g, nt, preferred_element_type=F32))
        for g in range(ATTN_KV_HEADS):
            vt_g = vt_ref[g, :, pl.ds(r0, kc)]
            for r in range(rep):
                h = g * rep + r
                s = sts[g][:, r * LANES:(r + 1) * LANES] + bias
                m_new = jnp.maximum(ms[h], jnp.max(s, axis=0, keepdims=True))
                alpha = jnp.exp(ms[h] - m_new)
                p = jnp.exp(s - m_new).astype(BF16)
                new_ms.append(m_new)
                pv = jnp.dot(vt_g, p, preferred_element_type=F32)
                rows = slice(h * V_ROWS, (h + 1) * V_ROWS)
                acc_ref[rows, :] = alpha * acc_ref[rows, :] + pv
        return tuple(new_ms)

    init = tuple(jnp.full((1, LANES), NEG_BIG, F32) for _ in range(ATTN_HEADS))
    lax.fori_loop(0, nch, attn_chunk, init)

    for h in range(ATTN_HEADS):
        num = acc_ref[h * V_ROWS:h * V_ROWS + HEAD_DIM, :]
        den = acc_ref[h * V_ROWS + HEAD_DIM:h * V_ROWS + HEAD_DIM + 1, :]
        ot_ref[h * HEAD_DIM:(h + 1) * HEAD_DIM, :] = num / den
    o_ref[...] = jnp.transpose(ot_ref[...]).astype(o_ref.dtype)


def _attention(a_pack, i_pack, idx_blk, vt, batch, seq, topk):
    nqb = seq // Q_BLOCK
    kc = min(KEY_CHUNK, seq)
    kern = functools.partial(_attn_kernel, topk=topk, kc=kc, halves=2 if seq % (2 * kc) == 0 else 1)
    return pl.pallas_call(
        kern,
        grid=(batch, nqb),
        in_specs=[
            pl.BlockSpec((Q_BLOCK, ATTN_W), lambda b, i: (b * nqb + i, 0)),
            pl.BlockSpec((Q_BLOCK, IDXQ_W), lambda b, i: (b * nqb + i, (ATTN_W + 2 * KV_W) // IDXQ_W)),
            pl.BlockSpec((seq, KV_W), lambda b, i: (b, ATTN_W // KV_W)),
            pl.BlockSpec((None, ATTN_KV_HEADS, V_ROWS, seq), lambda b, i: (b, 0, 0, 0)),
            pl.BlockSpec((seq, LANES), lambda b, i: (b, idx_blk)),
            pl.BlockSpec((Q_BLOCK, LANES), lambda b, i: (b * nqb + i, idx_blk)),
        ],
        out_specs=pl.BlockSpec((Q_BLOCK, ATTN_W), lambda b, i: (b * nqb + i, 0)),
        out_shape=jax.ShapeDtypeStruct((batch * seq, ATTN_W), BF16),
        scratch_shapes=[
            pltpu.VMEM((ATTN_HEADS * Q_BLOCK, HEAD_DIM), BF16),
            pltpu.VMEM((IDX_HEADS * Q_BLOCK, IDX_DIM), BF16),
            pltpu.VMEM((seq + kc, LANES), F32),
            pltpu.VMEM((seq, LANES), F32),
            pltpu.VMEM((ATTN_HEADS * V_ROWS, LANES), F32),
            pltpu.VMEM((ATTN_W, LANES), F32),
        ],
        compiler_params=_cparams(2),
        name="dsa_attention",
    )(a_pack, a_pack, a_pack, vt, i_pack, i_pack)


def _retconv_kernel(cb_ref, cc_ref, ch_ref, rq_ref, rk_ref, rv_ref, rg_ref,
                    cw_ref, cos_ref, sin_ref, dec_ref, conv_o, ret_o,
                    halo_ref, state_ref, *, ts):
    j = pl.program_id(1)

    @pl.when(j == 0)
    def _():
        halo_ref[...] = jnp.zeros_like(halo_ref)
        state_ref[...] = jnp.zeros_like(state_ref)

    u = cc_ref[...] * ch_ref[...]
    row = lax.broadcasted_iota(I32, u.shape, 0)
    prev1 = halo_ref[7:8, :]
    prev2 = halo_ref[6:7, :]
    u1 = jnp.where(row == 0, prev1, pltpu.roll(u, 1, 0))
    u2 = jnp.where(row == 0, prev2, jnp.where(row == 1, prev1, pltpu.roll(u, 2, 0)))
    y = cw_ref[0:1, :] * u2 + cw_ref[1:2, :] * u1 + cw_ref[2:3, :] * u
    conv_o[...] = (cb_ref[...] * y).astype(conv_o.dtype)
    halo_ref[...] = u[ts - 8:ts, :]

    cos = cos_ref[...]
    sin = sin_ref[...]
    half = RET_HEAD_DIM // 2
    nt = (((1,), (1,)), ((), ()))
    tn = (((0,), (0,)), ((), ()))
    for h in range(RET_HEADS):
        cols = slice(h * RET_HEAD_DIM, (h + 1) * RET_HEAD_DIM)
        qh = rq_ref[:, cols]
        kh = rk_ref[:, cols]
        qh = qh * cos + pltpu.roll(qh, half, 1) * sin
        kh = (kh * cos + pltpu.roll(kh, half, 1) * sin) * (RET_HEAD_DIM ** -0.5)
        vh = rv_ref[:, cols]
        intra = dec_ref[0, h]
        q_dec = dec_ref[1, h]
        k_dec = dec_ref[2, h]
        c_dec = dec_ref[3, h]
        state = state_ref[h]
        outs = []
        for c in range(ts // RET_CHUNK):
            rows = slice(c * RET_CHUNK, (c + 1) * RET_CHUNK)
            qc, kc_, vc = qh[rows], kh[rows], vh[rows].astype(BF16)
            s = lax.dot_general(qc.astype(BF16), kc_.astype(BF16), nt,
                                preferred_element_type=F32) * intra
            o = (jnp.dot(s.astype(BF16), vc, preferred_element_type=F32)
                 + jnp.dot((qc * q_dec).astype(BF16), state.astype(BF16),
                           preferred_element_type=F32))
            state = state * c_dec + lax.dot_general((kc_ * k_dec).astype(BF16), vc, tn,
                                                    preferred_element_type=F32)
            outs.append(o)
        state_ref[h] = state
        o = jnp.concatenate(outs, axis=0)
        mu = jnp.mean(o, axis=-1, keepdims=True)
        d = o - mu
        var = jnp.mean(d * d, axis=-1, keepdims=True)
        on = d * lax.rsqrt(var + LN_EPS)
        g = rg_ref[:, cols]
        ret_o[:, cols] = (on * (g / (1.0 + jnp.exp(-g)))).astype(ret_o.dtype)


def _retconv(r_pack, conv_w, cos_t, sin_t, dec, batch, seq):
    ts = min(SEQ_TILE, seq)
    nst = seq // ts
    t = batch * seq
    kern = functools.partial(_retconv_kernel, ts=ts)
    specs = [pl.BlockSpec((ts, CONV_DIM),
                          functools.partial(lambda b, j, n: (b * nst + j, n), n=n))
             for n in range(7)]
    return pl.pallas_call(
        kern,
        grid=(batch, nst),
        in_specs=specs + [
            pl.BlockSpec((8, CONV_DIM), lambda b, j: (0, 0)),
            pl.BlockSpec((ts, RET_HEAD_DIM), lambda b, j: (j, 0)),
            pl.BlockSpec((ts, RET_HEAD_DIM), lambda b, j: (j, 0)),
            pl.BlockSpec((4, RET_HEADS, RET_CHUNK, RET_CHUNK), lambda b, j: (0, 0, 0, 0)),
        ],
        out_specs=[pl.BlockSpec((ts, CONV_DIM), lambda b, j: (b * nst + j, 0)),
                   pl.BlockSpec((ts, RET_W), lambda b, j: (b * nst + j, 0))],
        out_shape=[jax.ShapeDtypeStruct((t, CONV_DIM), BF16),
                   jax.ShapeDtypeStruct((t, RET_W), BF16)],
        scratch_shapes=[pltpu.VMEM((8, CONV_DIM), F32),
                        pltpu.VMEM((RET_HEADS, RET_HEAD_DIM, RET_HEAD_DIM), F32)],
        compiler_params=_cparams(2),
        name="conv_retention",
    )(*([r_pack] * 7), conv_w, cos_t, sin_t, dec)


def _layer_norm(y, g, b):
    mu = jnp.mean(y, axis=-1, keepdims=True)
    d = y - mu
    var = jnp.mean(d * d, axis=-1, keepdims=True)
    return d * lax.rsqrt(var + LN_EPS) * g + b


def _outproj_kernel(attn_ref, conv_ref, ret_ref, x_ref, w_ref, g_ref, b_ref, rw_ref, rb_ref,
                    x1_ref, idx_ref, gate_ref, *, alpha):
    mix = jnp.dot(attn_ref[...], w_ref[0:ATTN_W, :], preferred_element_type=F32)
    mix = mix + jnp.dot(conv_ref[...], w_ref[ATTN_W:ATTN_W + CONV_DIM, :],
                        preferred_element_type=F32)
    mix = mix + jnp.dot(ret_ref[...], w_ref[ATTN_W + CONV_DIM:, :], preferred_element_type=F32)
    x1 = _layer_norm(alpha * x_ref[...] + mix, g_ref[...], b_ref[...])
    x1_ref[...] = x1

    nt = (((1,), (1,)), ((), ()))
    logits = lax.dot_general(rw_ref[...], x1.astype(BF16), nt,
                             preferred_element_type=F32) + rb_ref[...]
    eidx = lax.broadcasted_iota(I32, logits.shape, 0)
    vals, idxs = [], []
    for _ in range(TOP_K):
        mx = jnp.max(logits, axis=0, keepdims=True)
        am = jnp.min(jnp.where(logits == mx, eidx, N_EXPERTS), axis=0, keepdims=True)
        vals.append(mx)
        idxs.append(am)
        logits = jnp.where(eidx == am, -jnp.inf, logits)
    v = jnp.concatenate(vals, axis=0)
    e = jnp.exp(v - v[0:1, :])
    gates = e / jnp.sum(e, axis=0, keepdims=True)
    gate_ref[...] = jnp.concatenate([gates, jnp.zeros_like(gates)], axis=0)
    idx_ref[...] = jnp.concatenate(idxs, axis=0)


def _outproj_ln_router(attn, conv, ret, x, w_out, g, b, rw_t, rb, alpha):
    t, d = x.shape
    tm = min(ROW_TILE, t)
    kern = functools.partial(_outproj_kernel, alpha=alpha)
    row = lambda w: pl.BlockSpec((tm, w), lambda i: (i, 0))
    full = lambda s: pl.BlockSpec(s, lambda i: (0,) * len(s))
    return pl.pallas_call(
        kern,
        grid=(t // tm,),
        in_specs=[row(ATTN_W), row(CONV_DIM), row(RET_W), row(d), full((d, d)),
                  full((1, d)), full((1, d)), full((N_EXPERTS, d)), full((N_EXPERTS, 1))],
        out_specs=[row(d),
                   pl.BlockSpec((TOP_K, tm), lambda i: (0, i)),
                   pl.BlockSpec((2 * TOP_K, tm), lambda i: (0, i))],
        out_shape=[jax.ShapeDtypeStruct((t, d), F32),
                   jax.ShapeDtypeStruct((TOP_K, t), I32),
                   jax.ShapeDtypeStruct((2 * TOP_K, t), F32)],
        compiler_params=_cparams(1),
        name="outproj_ln_router",
    )(attn, conv, ret, x, w_out, g, b, rw_t, rb)


def _expert_kernel(te_ref, nu_ref, tokc_ref, tokn_ref, dst_ref, x_hbm,
                   wg_ref, wu_ref, wd_ref, bg_ref, bu_ref, bd_ref, out_hbm,
                   xbuf, ybuf, xb_ref, act_ref, gsem, ssem, *, tm):
    i = pl.program_id(0)
    n_used = nu_ref[0]
    slot = lax.rem(i, 2)
    d = xbuf.shape[2]
    n_real = out_hbm.shape[0] - 2 * tm

    def start_gather(tok_ref, s):
        def body(r, carry):
            pltpu.make_async_copy(x_hbm.at[pl.ds(tok_ref[0, 0, r], 1)],
                                  xbuf.at[s, pl.ds(r, 1)], gsem.at[s]).start()
            return carry
        lax.fori_loop(0, tm, body, 0, unroll=8)

    def wait_gather(s):
        pltpu.make_async_copy(x_hbm.at[pl.ds(0, tm)], xbuf.at[s], gsem.at[s]).wait()

    def wait_scatter(s):
        pltpu.make_async_copy(ybuf.at[s], out_hbm.at[pl.ds(0, tm)], ssem.at[s]).wait()

    @pl.when(i == 0)
    def _():
        ybuf[0] = jnp.zeros(ybuf.shape[1:], F32)
        for s in range(2):
            pltpu.make_async_copy(ybuf.at[0], out_hbm.at[pl.ds(n_real + s * tm, tm)],
                                  ssem.at[0]).start()
        for s in range(2):
            pltpu.make_async_copy(ybuf.at[0], out_hbm.at[pl.ds(n_real + s * tm, tm)],
                                  ssem.at[0]).wait()

    @pl.when((i == 0) & (n_used > 0))
    def _():
        start_gather(tokc_ref, 0)

    @pl.when(i + 1 < n_used)
    def _():
        start_gather(tokn_ref, 1 - slot)

    @pl.when(i < n_used)
    def _():
        wait_gather(slot)
        xb_ref[...] = xbuf[slot].astype(BF16)
        for c in range(D_FF // MXU_TILE):
            cols = slice(c * MXU_TILE, (c + 1) * MXU_TILE)
            gt = jnp.dot(xb_ref[...], wg_ref[:, cols], preferred_element_type=F32) + bg_ref[:, cols]
            up = jnp.dot(xb_ref[...], wu_ref[:, cols], preferred_element_type=F32) + bu_ref[:, cols]
            gt = jnp.minimum(gt, SWIGLU_LIMIT)
            up = jnp.clip(up, -SWIGLU_LIMIT, SWIGLU_LIMIT)
            act_ref[:, cols] = ((up + 1.0) * (gt / (1.0 + jnp.exp(-SWIGLU_ALPHA * gt)))).astype(BF16)
        for c in range(d // MXU_TILE):
            cols = slice(c * MXU_TILE, (c + 1) * MXU_TILE)
            ybuf[slot, :, cols] = (jnp.dot(act_ref[...], wd_ref[:, cols], preferred_element_type=F32)
                                   + bd_ref[:, cols])

        def body(r, carry):
            pltpu.make_async_copy(ybuf.at[slot, pl.ds(r, 1)],
                                  out_hbm.at[pl.ds(dst_ref[0, 0, r], 1)], ssem.at[slot]).start()
            return carry
        lax.fori_loop(0, tm, body, 0, unroll=8)

        @pl.when(i >= 1)
        def _():
            wait_scatter(1 - slot)

        @pl.when(i == n_used - 1)
        def _():
            wait_scatter(slot)


def _experts(x1, tile_expert, n_used, row_tok, row_dst, wg, wu, wd, bg, bu, bd, layer, tm):
    t, d = x1.shape
    n_tiles = tile_expert.shape[0]
    kern = functools.partial(_expert_kernel, tm=tm)
    smem_blk = lambda f: pl.BlockSpec((1, 1, tm), f, memory_space=pltpu.SMEM)
    wspec = lambda s: pl.BlockSpec((None,) + s, lambda i, te, nu: (layer * N_EXPERTS + te[i], 0, 0))
    grid_spec = pltpu.PrefetchScalarGridSpec(
        num_scalar_prefetch=2,
        grid=(n_tiles,),
        in_specs=[
            smem_blk(lambda i, te, nu: (i, 0, 0)),
            smem_blk(lambda i, te, nu: (jnp.minimum(i + 1, n_tiles - 1), 0, 0)),
            smem_blk(lambda i, te, nu: (i, 0, 0)),
            pl.BlockSpec(memory_space=pl.ANY),
            wspec((d, D_FF)), wspec((d, D_FF)), wspec((D_FF, d)),
            wspec((1, D_FF)), wspec((1, D_FF)), wspec((1, d)),
        ],
        out_specs=pl.BlockSpec(memory_space=pl.ANY),
        scratch_shapes=[pltpu.VMEM((2, tm, d), F32), pltpu.VMEM((2, tm, d), F32),
                        pltpu.VMEM((tm, d), BF16), pltpu.VMEM((tm, D_FF), BF16),
                        pltpu.SemaphoreType.DMA((2,)), pltpu.SemaphoreType.DMA((2,))],
    )
    return pl.pallas_call(
        kern,
        grid_spec=grid_spec,
        out_shape=jax.ShapeDtypeStruct((t * TOP_K + 2 * tm, d), F32),
        compiler_params=_cparams(1),
        name="expert_ffn",
    )(tile_expert, n_used, row_tok, row_tok, row_dst, x1, wg, wu, wd, bg, bu, bd)


def _row_table_kernel(first_ref, nvalid_ref, a_ref, tok_ref, dst_ref, *, tm, t, n_tiles):
    n_assign = TOP_K * t
    rows_per_tile = tm // LANES
    lane = lax.broadcasted_iota(I32, (rows_per_tile, LANES), 1)
    pos = lax.broadcasted_iota(I32, (rows_per_tile, LANES), 0) * LANES + lane

    def body(i, carry):
        first = first_ref[i]
        q = lax.shift_right_logical(first, 7)
        m = first & (LANES - 1)
        rows = a_ref[pl.ds(q, rows_per_tile + 1), :]
        rolled = pltpu.roll(rows, lax.rem(LANES - m, LANES), 1)
        a = jnp.where(lane + m < LANES, rolled[:rows_per_tile], rolled[1:])
        valid = pos < nvalid_ref[i]
        tok = a
        for k in range(1, TOP_K):
            tok = tok - jnp.where(a >= k * t, t, 0)
        out = pl.ds(i * rows_per_tile, rows_per_tile)
        tok_ref[out, :] = jnp.where(valid, tok, 0)
        dst_ref[out, :] = jnp.where(valid, a, n_assign + lax.rem(i, 2) * tm + pos)
        return carry

    lax.fori_loop(0, n_tiles, body, 0)


def _route(top_idx, tm):
    t = top_idx.shape[1]
    n_assign = t * TOP_K
    n_tiles = n_assign // tm + N_EXPERTS
    e_flat = top_idx.reshape(-1)
    ids = jnp.arange(n_assign, dtype=I32)
    _, a_sorted = lax.sort((e_flat, ids), num_keys=1, is_stable=True)
    counts = jnp.sum((e_flat[:, None] == jnp.arange(N_EXPERTS, dtype=I32)[None, :]).astype(I32), axis=0)
    padded = ((counts + tm - 1) // tm) * tm
    start = jnp.cumsum(counts) - counts
    pend = jnp.cumsum(padded)
    pstart = pend - padded
    n_used = (pend[-1] // tm).astype(I32)
    tile = jnp.arange(n_tiles, dtype=I32)
    te = jnp.sum((tile[:, None] * tm >= pend[None, :]).astype(I32), axis=1)
    last = jnp.sum(((n_used - 1) * tm >= pend).astype(I32))
    te = jnp.minimum(te, last)
    in_expert = tile * tm - pstart[te]
    n_valid = jnp.where(tile < n_used, jnp.clip(counts[te] - in_expert, 0, tm), 0).astype(I32)
    first = jnp.where(tile < n_used, start[te] + in_expert, 0).astype(I32)

    a2 = jnp.pad(a_sorted, (0, 8 * LANES)).reshape(-1, LANES)
    rows_out = n_tiles * tm // LANES
    kern = functools.partial(_row_table_kernel, tm=tm, t=t, n_tiles=n_tiles)
    table = jax.ShapeDtypeStruct((rows_out, LANES), I32)
    row_tok, row_dst = pl.pallas_call(
        kern,
        grid_spec=pltpu.PrefetchScalarGridSpec(
            num_scalar_prefetch=2,
            grid=(1,),
            in_specs=[pl.BlockSpec(a2.shape, lambda i, f, nv: (0, 0))],
            out_specs=[pl.BlockSpec((rows_out, LANES), lambda i, f, nv: (0, 0))] * 2,
        ),
        out_shape=[table, table],
        compiler_params=_cparams(1),
        name="row_tables",
    )(first, n_valid, a2)
    return (te, n_used.reshape(1), row_tok.reshape(n_tiles, 1, tm), row_dst.reshape(n_tiles, 1, tm))


def _combine_kernel(y0_ref, y1_ref, y2_ref, y3_ref, x_ref, gate_ref, g_ref, b_ref, o_ref, *, alpha):
    tm = x_ref.shape[0]
    cols = []
    for p in range(tm // LANES):
        piece = jnp.concatenate([gate_ref[:, p * LANES:(p + 1) * LANES],
                                 jnp.zeros((LANES - gate_ref.shape[0], LANES), F32)], axis=0)
        cols.append(jnp.transpose(piece))
    gates = [jnp.concatenate([c[:, k:k + 1] for c in cols], axis=0) for k in range(TOP_K)]
    ffn = ((gates[0] * y0_ref[...] + gates[1] * y1_ref[...])
           + (gates[2] * y2_ref[...] + gates[3] * y3_ref[...]))
    o_ref[...] = _layer_norm(alpha * x_ref[...] + ffn, g_ref[...], b_ref[...])


def _combine_ln(y4, x1, gates, g, b, alpha):
    t, d = x1.shape
    tm = min(ROW_TILE, t)
    nb = t // tm
    kern = functools.partial(_combine_kernel, alpha=alpha)
    slot_specs = [pl.BlockSpec((tm, d), functools.partial(lambda i, k: (k * nb + i, 0), k=k))
                  for k in range(TOP_K)]
    return pl.pallas_call(
        kern,
        grid=(nb,),
        in_specs=slot_specs + [pl.BlockSpec((tm, d), lambda i: (i, 0)),
                               pl.BlockSpec((gates.shape[0], tm), lambda i: (0, i)),
                               pl.BlockSpec((1, d), lambda i: (0, 0)),
                               pl.BlockSpec((1, d), lambda i: (0, 0))],
        out_specs=pl.BlockSpec((tm, d), lambda i: (i, 0)),
        out_shape=jax.ShapeDtypeStruct((t, d), F32),
        compiler_params=_cparams(1),
        name="combine_ln",
    )(y4, y4, y4, y4, x1, gates, g, b)


def _deinterleave_kernel(w_ref, g_ref, u_ref):
    half = MXU_TILE // 2
    src = lax.broadcasted_iota(I32, (MXU_TILE, MXU_TILE), 0)
    dst = lax.broadcasted_iota(I32, (MXU_TILE, MXU_TILE), 1)
    perm = jnp.where(src == jnp.where(dst < half, 2 * dst, 2 * (dst - half) + 1), 1.0, 0.0).astype(BF16)
    for c in range(w_ref.shape[1] // MXU_TILE):
        blk = w_ref[:, c * MXU_TILE:(c + 1) * MXU_TILE].astype(BF16)
        sp = jnp.dot(blk, perm, preferred_element_type=F32)
        g_ref[:, c * half:(c + 1) * half] = sp[:, :half].astype(BF16)
        u_ref[:, c * half:(c + 1) * half] = sp[:, half:].astype(BF16)


def _deinterleave(w_gu):
    n, d, f2 = w_gu.shape
    tr = min(1024, d)
    out = jax.ShapeDtypeStruct((n, d, f2 // 2), BF16)
    return pl.pallas_call(
        _deinterleave_kernel,
        grid=(n, d // tr),
        in_specs=[pl.BlockSpec((None, tr, f2), lambda e, r: (e, r, 0))],
        out_specs=[pl.BlockSpec((None, tr, f2 // 2), lambda e, r: (e, r, 0))] * 2,
        out_shape=[out, out],
        compiler_params=_cparams(2),
        name="split_gate_up",
    )(w_gu)


def _rotary_tables(seq):
    half = RET_HEAD_DIM // 2
    inv_freq = ROPE_BASE ** (-jnp.linspace(0.0, 1.0, half, dtype=F32))
    ang = jnp.arange(seq).astype(F32)[:, None] * inv_freq[None, :]
    cos, sin = jnp.cos(ang), jnp.sin(ang)
    return (jnp.concatenate([cos, cos], axis=-1), jnp.concatenate([-sin, sin], axis=-1))


def _decay_tables():
    c = RET_CHUNK
    lg = jnp.log1p(-jnp.exp2(-5.0 - jnp.arange(RET_HEADS, dtype=F32)))
    i = jnp.arange(c, dtype=F32)
    diff = i[:, None] - i[None, :]
    intra = jnp.where(diff[None] >= 0, jnp.exp(jnp.maximum(diff, 0.0)[None] * lg[:, None, None]), 0.0)
    q_dec = jnp.exp((i + 1.0)[None, :] * lg[:, None])[..., None]
    k_dec = jnp.exp((c - 1.0 - i)[None, :] * lg[:, None])[..., None]
    c_dec = jnp.exp(c * lg)[:, None, None]
    full = (RET_HEADS, c, c)
    return jnp.stack([intra.astype(F32), jnp.broadcast_to(q_dec, full),
                      jnp.broadcast_to(k_dec, full), jnp.broadcast_to(c_dec, full)])


def _split_w_in(w_in):
    pts = np.cumsum((0,) + IN_SIZES)
    a = w_in[..., pts[0]:pts[4]]
    idx = w_in[..., pts[4]:pts[6]]
    idx = jnp.pad(idx, ((0, 0), (0, 0), (0, LANES - idx.shape[-1])))
    rest = jnp.concatenate([w_in[..., pts[6]:], idx], axis=-1)
    return a.astype(BF16), rest.astype(BF16)


def _cast_kernel(x_ref, o_ref):
    o_ref[...] = x_ref[...].astype(o_ref.dtype)


def _to_bf16(w):
    n, r, c = w.shape
    blk = (2 if n % 2 == 0 else 1, r, c)
    return pl.pallas_call(
        _cast_kernel,
        grid=(n // blk[0],),
        in_specs=[pl.BlockSpec(blk, lambda i: (i, 0, 0))],
        out_specs=pl.BlockSpec(blk, lambda i: (i, 0, 0)),
        out_shape=jax.ShapeDtypeStruct(w.shape, BF16),
        compiler_params=_cparams(1),
        name="cast_bf16",
    )(w)


def _v_transposed(a_pack, batch, seq):
    v = a_pack[:, ATTN_W + KV_W:ATTN_W + 2 * KV_W].reshape(batch, seq, ATTN_KV_HEADS, HEAD_DIM)
    vt = jnp.transpose(v, (0, 2, 3, 1))
    ones = jnp.ones((batch, ATTN_KV_HEADS, BF16_ROWS, seq), BF16)
    return jnp.concatenate([vt, ones], axis=2)


def kernel(x, w_in, conv_w, w_out, ln1_g, ln1_b, router_w, router_b,
           w_gu, b_gu, w_down, b_down, ln2_g, ln2_b):
    batch, seq, d = x.shape
    depth = w_in.shape[0]
    t = batch * seq
    alpha = (2 * depth) ** 0.25
    topk = min(IDX_TOPK_MAX, seq // 4)
    tm_e = EXPERT_TILE

    wa, wr = _split_w_in(w_in)
    idx_blk = (wr.shape[-1] - LANES) // LANES
    w_out_b = w_out.astype(BF16)
    rw_t = jnp.swapaxes(router_w, 1, 2).astype(BF16)
    n_exp = depth * N_EXPERTS
    wg, wu = _deinterleave(w_gu.reshape(n_exp, d, 2 * D_FF))
    wd = _to_bf16(w_down.reshape(n_exp, D_FF, d))
    bg = b_gu[..., 0::2].reshape(n_exp, 1, D_FF)
    bu = b_gu[..., 1::2].reshape(n_exp, 1, D_FF)
    bd = b_down.reshape(n_exp, 1, d)
    cw = jnp.pad(conv_w, ((0, 0), (0, 8 - CONV_WIDTH), (0, 0)))
    cos_t, sin_t = _rotary_tables(seq)
    dec = _decay_tables()

    xf = x.reshape(t, d)
    for l in range(depth):
        a_pack = _matmul(xf, wa[l], min(512, t), 1024, BF16, "in_proj_attn")
        r_pack = _matmul(xf, wr[l], min(ROW_TILE, t), wr.shape[-1], F32, "in_proj_rest")
        attn = _attention(a_pack, r_pack, idx_blk, _v_transposed(a_pack, batch, seq), batch, seq, topk)
        conv, ret = _retconv(r_pack, cw[l], cos_t, sin_t, dec, batch, seq)
        x1, top_idx, gates = _outproj_ln_router(
            attn, conv, ret, xf, w_out_b[l], ln1_g[l][None], ln1_b[l][None],
            rw_t[l], router_b[l][:, None], alpha)
        te, n_used, row_tok, row_dst = _route(top_idx, tm_e)
        y4 = _experts(x1, te, n_used, row_tok, row_dst, wg, wu, wd, bg, bu, bd, l, tm_e)
        xf = _combine_ln(y4, x1, gates, ln2_g[l][None], ln2_b[l][None], alpha)
    return xf.reshape(batch, seq, d)
```

```python
import functools

import jax
import jax.numpy as jnp
import numpy as np
from jax import lax
from jax.experimental import pallas as pl
from jax.experimental.pallas import tpu as pltpu

F32 = jnp.float32
BF16 = jnp.bfloat16
I32 = jnp.int32

ATTN_HEADS = 16
ATTN_KV_HEADS = 4
HEAD_DIM = 64
IDX_HEADS = 8
IDX_DIM = 64
IDX_TOPK_MAX = 256
Q_BLOCK = 128
CONV_DIM = 512
CONV_WIDTH = 3
RET_HEADS = 4
RET_HEAD_DIM = 128
RET_CHUNK = 128
ROPE_BASE = 10000.0
N_EXPERTS = 32
TOP_K = 4
D_FF = 768
SWIGLU_LIMIT = 7.0
SWIGLU_ALPHA = 1.702
LN_EPS = 1e-5

ATTN_W = ATTN_HEADS * HEAD_DIM
KV_W = ATTN_KV_HEADS * HEAD_DIM
IDXQ_W = IDX_HEADS * IDX_DIM
RET_W = RET_HEADS * RET_HEAD_DIM
IN_SIZES = (ATTN_W, KV_W, KV_W, IDXQ_W, IDX_DIM, IDX_HEADS,
            CONV_DIM, CONV_DIM, CONV_DIM, RET_W, RET_W, RET_W, RET_W)

LANES = 128
MXU_TILE = 256
BF16_ROWS = 16
KEY_CHUNK = 512
SEQ_TILE = 512
ROW_TILE = 256
EXPERT_TILE = 256
VMEM_LIMIT = 56 * 1024 * 1024

NEG_BIG = -1e30
LOG2_E = 1.4426950408889634
V_ROWS = HEAD_DIM + BF16_ROWS
BISECT_STEPS = 20
BISECT_EXTRA = 4
BISECT_MAX_ROUNDS = 80


def _cparams(n_axes):
    return pltpu.CompilerParams(dimension_semantics=("arbitrary",) * n_axes,
                                vmem_limit_bytes=VMEM_LIMIT)


def _mm_kernel(x_ref, w_ref, o_ref, *, first_block_scale):
    acc = jnp.dot(x_ref[...].astype(BF16), w_ref[...], preferred_element_type=F32)
    if first_block_scale is not None:
        acc = acc * jnp.where(pl.program_id(0) == 0, first_block_scale, 1.0)
    o_ref[...] = acc.astype(o_ref.dtype)


def _matmul(x, w, tm, tn, out_dtype, name, first_block_scale=None):
    m, k = x.shape
    n = w.shape[1]
    return pl.pallas_call(
        functools.partial(_mm_kernel, first_block_scale=first_block_scale),
        grid=(n // tn, m // tm),
        in_specs=[pl.BlockSpec((tm, k), lambda j, i: (i, 0)),
                  pl.BlockSpec((k, tn), lambda j, i: (0, j))],
        out_specs=pl.BlockSpec((tm, tn), lambda j, i: (i, j)),
        out_shape=jax.ShapeDtypeStruct((m, n), out_dtype),
        compiler_params=_cparams(2),
        name=name,
    )(x, w)


def _attn_kernel(q_ref, qi_ref, k_ref, vt_ref, ki_ref, iw_ref, o_ref,
                 qs_ref, qis_ref, s_ref, bias_ref, acc_ref, ot_ref, *, topk, kc, halves):
    i = pl.program_id(1)
    q0 = i * Q_BLOCK
    nch = (q0 + Q_BLOCK + kc - 1) // kc
    kc2 = halves * kc
    nch2 = (nch + halves - 1) // halves
    nt = (((1,), (1,)), ((), ()))

    for h in range(ATTN_HEADS):
        qs_ref[h * Q_BLOCK:(h + 1) * Q_BLOCK, :] = q_ref[:, h * HEAD_DIM:(h + 1) * HEAD_DIM]
    for h in range(IDX_HEADS):
        qis_ref[h * Q_BLOCK:(h + 1) * Q_BLOCK, :] = qi_ref[:, h * IDX_DIM:(h + 1) * IDX_DIM]

    wt = jnp.transpose(iw_ref[...])[IDX_DIM:IDX_DIM + IDX_HEADS, :] * (IDXQ_W ** -0.5)
    qpos = q0 + lax.broadcasted_iota(I32, (kc, LANES), 1)
    krow = lax.broadcasted_iota(I32, (kc, LANES), 0)

    def score_chunk(c, carry):
        lo, hi = carry
        for half in range(halves):
            r0 = pl.multiple_of(c * kc2 + half * kc, kc)
            ki_c = ki_ref[pl.ds(r0, kc), :][:, :IDX_DIM].astype(BF16)
            r = lax.dot_general(ki_c, qis_ref[...], nt, preferred_element_type=F32)
            sc = jnp.zeros((kc, LANES), F32)
            for h in range(IDX_HEADS):
                sc = sc + jnp.maximum(r[:, h * LANES:(h + 1) * LANES], 0.0) * wt[h:h + 1, :]
            causal = r0 + krow <= qpos
            s_ref[pl.ds(r0, kc), :] = jnp.where(causal, sc, -jnp.inf)
            lo = jnp.minimum(lo, jnp.min(jnp.where(causal, sc, jnp.inf), axis=0, keepdims=True))
            hi = jnp.maximum(hi, jnp.max(jnp.where(causal, sc, -jnp.inf), axis=0, keepdims=True))
        return lo, hi

    lo0, hi0 = lax.fori_loop(0, nch2, score_chunk,
                             (jnp.full((1, LANES), jnp.inf, F32), jnp.full((1, LANES), -jnp.inf, F32)))

    def column_sum(vals):
        acc = lax.fori_loop(0, nch, lambda c, a: a + vals(s_ref[pl.ds(pl.multiple_of(c * kc, kc), kc), :]),
                            jnp.zeros((8, LANES), I32))
        return jnp.sum(acc, axis=0, keepdims=True)

    def count(pred):
        return column_sum(lambda s: jnp.sum(jnp.where(pred(s), 1, 0).reshape(kc // 8, 8, LANES), axis=0))

    def bisect(_, carry):
        lo, hi = carry
        mid = lo + (hi - lo) * 0.5
        mid = jnp.where(mid == lo, hi, mid)
        ge = count(lambda s: s >= mid) >= topk
        return jnp.where(ge, mid, lo), jnp.where(ge, hi, mid)

    def settle(lo):
        def body(c, m):
            s = s_ref[pl.ds(pl.multiple_of(c * kc, kc), kc), :]
            return jnp.minimum(m, jnp.min(jnp.where(s >= lo, s, jnp.inf).reshape(kc // 8, 8, LANES), axis=0))
        m = lax.fori_loop(0, nch, body, jnp.full((8, LANES), jnp.inf, F32))
        thr = jnp.min(m, axis=0, keepdims=True)
        return thr, count(lambda s: s > thr)

    lo, hi = lax.fori_loop(0, BISECT_STEPS, bisect, (lo0, hi0))
    thr, cnt_gt = settle(lo)

    def unsettled(state):
        _, _, _, cnt_gt, rounds = state
        return (jnp.max(cnt_gt) >= topk) & (rounds < BISECT_MAX_ROUNDS)

    def refine(state):
        lo, hi, _, _, rounds = state
        lo, hi = lax.fori_loop(0, BISECT_EXTRA, bisect, (lo, hi))
        thr, cnt_gt = settle(lo)
        return lo, hi, thr, cnt_gt, rounds + 1

    _, _, thr, cnt_gt, _ = lax.while_loop(unsettled, refine, (lo, hi, thr, cnt_gt, jnp.int32(0)))
    cnt_eq = count(lambda s: s == thr)
    need = topk - cnt_gt
    has_ties = jnp.max(jnp.where(cnt_eq > need, 1, 0)) > 0

    @pl.when(jnp.logical_not(has_ties))
    def _():
        def body(c, carry):
            r0 = pl.multiple_of(c * kc, kc)
            s = s_ref[pl.ds(r0, kc), :]
            bias_ref[pl.ds(r0, kc), :] = jnp.where(s >= thr, 0.0, NEG_BIG)
            return carry
        lax.fori_loop(0, nch, body, 0)

    @pl.when(has_ties)
    def _():
        tri = (lax.broadcasted_iota(I32, (kc, kc), 1)
               < lax.broadcasted_iota(I32, (kc, kc), 0)).astype(BF16)
        need_f = need.astype(F32)

        def body(c, seen):
            r0 = pl.multiple_of(c * kc, kc)
            s = s_ref[pl.ds(r0, kc), :]
            eq = s == thr
            eq_f = jnp.where(eq, 1.0, 0.0)
            before = jnp.dot(tri, eq_f.astype(BF16), preferred_element_type=F32) + seen
            sel = (s > thr) | (eq & (before < need_f))
            bias_ref[pl.ds(r0, kc), :] = jnp.where(sel, 0.0, NEG_BIG)
            return seen + jnp.sum(eq_f, axis=0, keepdims=True)
        lax.fori_loop(0, nch, body, jnp.zeros((1, LANES), F32))

    acc_ref[...] = jnp.zeros_like(acc_ref)
    rep = ATTN_HEADS // ATTN_KV_HEADS

    def attn_chunk(c, ms):
        r0 = pl.multiple_of(c * kc, kc)
        bias = bias_ref[pl.ds(r0, kc), :]
        k_c = k_ref[pl.ds(r0, kc), :]
        new_ms = []
        sts = []
        for g in range(ATTN_KV_HEADS):
            k_g = k_c[:, g * HEAD_DIM:(g + 1) * HEAD_DIM]
            qs_g = qs_ref[g * rep * Q_BLOCK:(g + 1) * rep * Q_BLOCK, :]
            sts.append(lax.dot_general(k_g, qs_g, nt, preferred_element_type=F32))
        for g in range(ATTN_KV_HEADS):
            vt_g = vt_ref[g, :, pl.ds(r0, kc)]
            for r in range(rep):
                h = g * rep + r
                s = sts[g][:, r * LANES:(r + 1) * LANES] + bias
                m_new = jnp.maximum(ms[h], jnp.max(s, axis=0, keepdims=True))
                alpha = jnp.exp2(ms[h] - m_new)
                p = jnp.exp2(s - m_new).astype(BF16)
                new_ms.append(m_new)
                pv = jnp.dot(vt_g, p, preferred_element_type=F32)
                rows = slice(h * V_ROWS, (h + 1) * V_ROWS)
                acc_ref[rows, :] = alpha * acc_ref[rows, :] + pv
        return tuple(new_ms)

    init = tuple(jnp.full((1, LANES), NEG_BIG, F32) for _ in range(ATTN_HEADS))
    lax.fori_loop(0, nch, attn_chunk, init)

    for h in range(ATTN_HEADS):
        num = acc_ref[h * V_ROWS:h * V_ROWS + HEAD_DIM, :]
        den = acc_ref[h * V_ROWS + HEAD_DIM:h * V_ROWS + HEAD_DIM + 1, :]
        ot_ref[h * HEAD_DIM:(h + 1) * HEAD_DIM, :] = num / den
    o_ref[...] = jnp.transpose(ot_ref[...]).astype(o_ref.dtype)


def _attention(a_pack, i_pack, idx_blk, vt, batch, seq, topk):
    nqb = seq // Q_BLOCK
    kc = min(KEY_CHUNK, seq)
    kern = functools.partial(_attn_kernel, topk=topk, kc=kc, halves=2 if seq % (2 * kc) == 0 else 1)
    return pl.pallas_call(
        kern,
        grid=(batch, nqb),
        in_specs=[
            pl.BlockSpec((Q_BLOCK, ATTN_W), lambda b, i: (b * nqb + i, 0)),
            pl.BlockSpec((Q_BLOCK, IDXQ_W), lambda b, i: (b * nqb + i, (ATTN_W + 2 * KV_W) // IDXQ_W)),
            pl.BlockSpec((seq, KV_W), lambda b, i: (b, ATTN_W // KV_W)),
            pl.BlockSpec((None, ATTN_KV_HEADS, V_ROWS, seq), lambda b, i: (b, 0, 0, 0)),
            pl.BlockSpec((seq, LANES), lambda b, i: (b, idx_blk)),
            pl.BlockSpec((Q_BLOCK, LANES), lambda b, i: (b * nqb + i, idx_blk)),
        ],
        out_specs=pl.BlockSpec((Q_BLOCK, ATTN_W), lambda b, i: (b * nqb + i, 0)),
        out_shape=jax.ShapeDtypeStruct((batch * seq, ATTN_W), BF16),
        scratch_shapes=[
            pltpu.VMEM((ATTN_HEADS * Q_BLOCK, HEAD_DIM), BF16),
            pltpu.VMEM((IDX_HEADS * Q_BLOCK, IDX_DIM), BF16),
            pltpu.VMEM((seq + kc, LANES), F32),
            pltpu.VMEM((seq, LANES), F32),
            pltpu.VMEM((ATTN_HEADS * V_ROWS, LANES), F32),
            pltpu.VMEM((ATTN_W, LANES), F32),
        ],
        compiler_params=_cparams(2),
        name="dsa_attention",
    )(a_pack, a_pack, a_pack, vt, i_pack, i_pack)


def _retconv_kernel(cb_ref, cc_ref, ch_ref, rq_ref, rk_ref, rv_ref, rg_ref,
                    cw_ref, cos_ref, sin_ref, dec_ref, conv_o, ret_o,
                    halo_ref, state_ref, *, ts):
    j = pl.program_id(1)

    @pl.when(j == 0)
    def _():
        halo_ref[...] = jnp.zeros_like(halo_ref)
        state_ref[...] = jnp.zeros_like(state_ref)

    u = cc_ref[...] * ch_ref[...]
    row = lax.broadcasted_iota(I32, u.shape, 0)
    prev1 = halo_ref[7:8, :]
    prev2 = halo_ref[6:7, :]
    u1 = jnp.where(row == 0, prev1, pltpu.roll(u, 1, 0))
    u2 = jnp.where(row == 0, prev2, jnp.where(row == 1, prev1, pltpu.roll(u, 2, 0)))
    y = cw_ref[0:1, :] * u2 + cw_ref[1:2, :] * u1 + cw_ref[2:3, :] * u
    conv_o[...] = (cb_ref[...] * y).astype(conv_o.dtype)
    halo_ref[...] = u[ts - 8:ts, :]

    cos = cos_ref[...]
    sin = sin_ref[...]
    half = RET_HEAD_DIM // 2
    nt = (((1,), (1,)), ((), ()))
    tn = (((0,), (0,)), ((), ()))
    for h in range(RET_HEADS):
        cols = slice(h * RET_HEAD_DIM, (h + 1) * RET_HEAD_DIM)
        qh = rq_ref[:, cols]
        kh = rk_ref[:, cols]
        qh = qh * cos + pltpu.roll(qh, half, 1) * sin
        kh = (kh * cos + pltpu.roll(kh, half, 1) * sin) * (RET_HEAD_DIM ** -0.5)
        vh = rv_ref[:, cols]
        intra = dec_ref[0, h]
        q_dec = dec_ref[1, h]
        k_dec = dec_ref[2, h]
        c_dec = dec_ref[3, h]
        state = state_ref[h]
        outs = []
        for c in range(ts // RET_CHUNK):
            rows = slice(c * RET_CHUNK, (c + 1) * RET_CHUNK)
            qc, kc_, vc = qh[rows], kh[rows], vh[rows].astype(BF16)
            s = lax.dot_general(qc.astype(BF16), kc_.astype(BF16), nt,
                                preferred_element_type=F32) * intra
            o = (jnp.dot(s.astype(BF16), vc, preferred_element_type=F32)
                 + jnp.dot((qc * q_dec).astype(BF16), state.astype(BF16),
                           preferred_element_type=F32))
            state = state * c_dec + lax.dot_general((kc_ * k_dec).astype(BF16), vc, tn,
                                                    preferred_element_type=F32)
            outs.append(o)
        state_ref[h] = state
        o = jnp.concatenate(outs, axis=0)
        mu = jnp.mean(o, axis=-1, keepdims=True)
        d = o - mu
        var = jnp.mean(d * d, axis=-1, keepdims=True)
        on = d * lax.rsqrt(var + LN_EPS)
        g = rg_ref[:, cols]
        ret_o[:, cols] = (on * (g / (1.0 + jnp.exp(-g)))).astype(ret_o.dtype)


def _retconv(r_pack, conv_w, cos_t, sin_t, dec, batch, seq):
    ts = min(SEQ_TILE, seq)
    nst = seq // ts
    t = batch * seq
    kern = functools.partial(_retconv_kernel, ts=ts)
    specs = [pl.BlockSpec((ts, CONV_DIM),
                          functools.partial(lambda b, j, n: (b * nst + j, n), n=n))
             for n in range(7)]
    return pl.pallas_call(
        kern,
        grid=(batch, nst),
        in_specs=specs + [
            pl.BlockSpec((8, CONV_DIM), lambda b, j: (0, 0)),
            pl.BlockSpec((ts, RET_HEAD_DIM), lambda b, j: (j, 0)),
            pl.BlockSpec((ts, RET_HEAD_DIM), lambda b, j: (j, 0)),
            pl.BlockSpec((4, RET_HEADS, RET_CHUNK, RET_CHUNK), lambda b, j: (0, 0, 0, 0)),
        ],
        out_specs=[pl.BlockSpec((ts, CONV_DIM), lambda b, j: (b * nst + j, 0)),
                   pl.BlockSpec((ts, RET_W), lambda b, j: (b * nst + j, 0))],
        out_shape=[jax.ShapeDtypeStruct((t, CONV_DIM), BF16),
                   jax.ShapeDtypeStruct((t, RET_W), BF16)],
        scratch_shapes=[pltpu.VMEM((8, CONV_DIM), F32),
                        pltpu.VMEM((RET_HEADS, RET_HEAD_DIM, RET_HEAD_DIM), F32)],
        compiler_params=_cparams(2),
        name="conv_retention",
    )(*([r_pack] * 7), conv_w, cos_t, sin_t, dec)


def _layer_norm(y, g, b):
    mu = jnp.mean(y, axis=-1, keepdims=True)
    d = y - mu
    var = jnp.mean(d * d, axis=-1, keepdims=True)
    return d * lax.rsqrt(var + LN_EPS) * g + b


def _outproj_kernel(attn_ref, conv_ref, ret_ref, x_ref, w_ref, g_ref, b_ref, rw_ref, rb_ref,
                    x1_ref, idx_ref, gate_ref, cnt_ref, *, alpha):
    mix = jnp.dot(attn_ref[...], w_ref[0:ATTN_W, :], preferred_element_type=F32)
    mix = mix + jnp.dot(conv_ref[...], w_ref[ATTN_W:ATTN_W + CONV_DIM, :],
                        preferred_element_type=F32)
    mix = mix + jnp.dot(ret_ref[...], w_ref[ATTN_W + CONV_DIM:, :], preferred_element_type=F32)
    x1 = _layer_norm(alpha * x_ref[...] + mix, g_ref[...], b_ref[...])
    x1_ref[...] = x1

    nt = (((1,), (1,)), ((), ()))
    logits = lax.dot_general(rw_ref[...], x1.astype(BF16), nt,
                             preferred_element_type=F32) + rb_ref[...]
    eidx = lax.broadcasted_iota(I32, logits.shape, 0)
    vals, idxs = [], []
    chosen = jnp.zeros(logits.shape, I32)
    for _ in range(TOP_K):
        mx = jnp.max(logits, axis=0, keepdims=True)
        am = jnp.min(jnp.where(logits == mx, eidx, N_EXPERTS), axis=0, keepdims=True)
        vals.append(mx)
        idxs.append(am)
        chosen = chosen + jnp.where(eidx == am, 1, 0)
        logits = jnp.where(eidx == am, -jnp.inf, logits)

    @pl.when(pl.program_id(0) == 0)
    def _():
        cnt_ref[...] = jnp.zeros_like(cnt_ref)
    part = chosen[:, 0:LANES]
    for p in range(1, chosen.shape[1] // LANES):
        part = part + chosen[:, p * LANES:(p + 1) * LANES]
    cnt_ref[...] += part
    v = jnp.concatenate(vals, axis=0)
    e = jnp.exp(v - v[0:1, :])
    gates = e / jnp.sum(e, axis=0, keepdims=True)
    gate_ref[...] = jnp.concatenate([gates, jnp.zeros_like(gates)], axis=0)
    idx_ref[...] = jnp.concatenate(idxs, axis=0)


def _outproj_ln_router(attn, conv, ret, x, w_out, g, b, rw_t, rb, alpha):
    t, d = x.shape
    tm = min(ROW_TILE, t)
    kern = functools.partial(_outproj_kernel, alpha=alpha)
    row = lambda w: pl.BlockSpec((tm, w), lambda i: (i, 0))
    full = lambda s: pl.BlockSpec(s, lambda i: (0,) * len(s))
    return pl.pallas_call(
        kern,
        grid=(t // tm,),
        in_specs=[row(ATTN_W), row(CONV_DIM), row(RET_W), row(d), full((d, d)),
                  full((1, d)), full((1, d)), full((N_EXPERTS, d)), full((N_EXPERTS, 1))],
        out_specs=[row(d),
                   pl.BlockSpec((TOP_K, tm), lambda i: (0, i)),
                   pl.BlockSpec((2 * TOP_K, tm), lambda i: (0, i)),
                   pl.BlockSpec((N_EXPERTS, LANES), lambda i: (0, 0))],
        out_shape=[jax.ShapeDtypeStruct((t, d), F32),
                   jax.ShapeDtypeStruct((TOP_K, t), I32),
                   jax.ShapeDtypeStruct((2 * TOP_K, t), F32),
                   jax.ShapeDtypeStruct((N_EXPERTS, LANES), I32)],
        compiler_params=_cparams(1),
        name="outproj_ln_router",
    )(attn, conv, ret, x, w_out, g, b, rw_t, rb)


def _expert_kernel(te_ref, nu_ref, tokc_ref, tokn_ref, dst_ref, x_hbm,
                   wg_ref, wu_ref, wd_ref, bg_ref, bu_ref, bd_ref, out_hbm,
                   xbuf, ybuf, xb_ref, act_ref, gsem, ssem, *, tm):
    i = pl.program_id(0)
    n_used = nu_ref[0]
    slot = lax.rem(i, 2)
    d = xbuf.shape[2]
    n_real = out_hbm.shape[0] - 2 * tm

    def start_gather(tok_ref, s):
        def body(r, carry):
            pltpu.make_async_copy(x_hbm.at[pl.ds(tok_ref[0, 0, r], 1)],
                                  xbuf.at[s, pl.ds(r, 1)], gsem.at[s]).start()
            return carry
        lax.fori_loop(0, tm, body, 0, unroll=8)

    def wait_gather(s):
        pltpu.make_async_copy(x_hbm.at[pl.ds(0, tm)], xbuf.at[s], gsem.at[s]).wait()

    def wait_scatter(s):
        pltpu.make_async_copy(ybuf.at[s], out_hbm.at[pl.ds(0, tm)], ssem.at[s]).wait()

    @pl.when(i == 0)
    def _():
        ybuf[0] = jnp.zeros(ybuf.shape[1:], F32)
        for s in range(2):
            pltpu.make_async_copy(ybuf.at[0], out_hbm.at[pl.ds(n_real + s * tm, tm)],
                                  ssem.at[0]).start()
        for s in range(2):
            pltpu.make_async_copy(ybuf.at[0], out_hbm.at[pl.ds(n_real + s * tm, tm)],
                                  ssem.at[0]).wait()

    @pl.when((i == 0) & (n_used > 0))
    def _():
        start_gather(tokc_ref, 0)

    @pl.when(i + 1 < n_used)
    def _():
        start_gather(tokn_ref, 1 - slot)

    @pl.when(i < n_used)
    def _():
        wait_gather(slot)
        xb_ref[...] = xbuf[slot].astype(BF16)
        for c in range(D_FF // MXU_TILE):
            cols = slice(c * MXU_TILE, (c + 1) * MXU_TILE)
            gt = jnp.dot(xb_ref[...], wg_ref[:, cols], preferred_element_type=F32) + bg_ref[:, cols]
            up = jnp.dot(xb_ref[...], wu_ref[:, cols], preferred_element_type=F32) + bu_ref[:, cols]
            gt = jnp.minimum(gt, SWIGLU_LIMIT)
            up = jnp.clip(up, -SWIGLU_LIMIT, SWIGLU_LIMIT)
            act_ref[:, cols] = ((up + 1.0) * (gt / (1.0 + jnp.exp(-SWIGLU_ALPHA * gt)))).astype(BF16)
        for c in range(d // MXU_TILE):
            cols = slice(c * MXU_TILE, (c + 1) * MXU_TILE)
            ybuf[slot, :, cols] = (jnp.dot(act_ref[...], wd_ref[:, cols], preferred_element_type=F32)
                                   + bd_ref[:, cols])

        def body(r, carry):
            pltpu.make_async_copy(ybuf.at[slot, pl.ds(r, 1)],
                                  out_hbm.at[pl.ds(dst_ref[0, 0, r], 1)], ssem.at[slot]).start()
            return carry
        lax.fori_loop(0, tm, body, 0, unroll=8)

        @pl.when(i >= 1)
        def _():
            wait_scatter(1 - slot)

        @pl.when(i == n_used - 1)
        def _():
            wait_scatter(slot)


def _experts(x1, tile_expert, n_used, row_tok, row_dst, wg, wu, wd, bg, bu, bd, layer, tm):
    t, d = x1.shape
    n_tiles = tile_expert.shape[0]
    kern = functools.partial(_expert_kernel, tm=tm)
    smem_blk = lambda f: pl.BlockSpec((1, 1, tm), f, memory_space=pltpu.SMEM)
    wspec = lambda s: pl.BlockSpec((None,) + s, lambda i, te, nu: (layer * N_EXPERTS + te[i], 0, 0))
    grid_spec = pltpu.PrefetchScalarGridSpec(
        num_scalar_prefetch=2,
        grid=(n_tiles,),
        in_specs=[
            smem_blk(lambda i, te, nu: (i, 0, 0)),
            smem_blk(lambda i, te, nu: (jnp.minimum(i + 1, n_tiles - 1), 0, 0)),
            smem_blk(lambda i, te, nu: (i, 0, 0)),
            pl.BlockSpec(memory_space=pl.ANY),
            wspec((d, D_FF)), wspec((d, D_FF)), wspec((D_FF, d)),
            wspec((1, D_FF)), wspec((1, D_FF)), wspec((1, d)),
        ],
        out_specs=pl.BlockSpec(memory_space=pl.ANY),
        scratch_shapes=[pltpu.VMEM((2, tm, d), F32), pltpu.VMEM((2, tm, d), F32),
                        pltpu.VMEM((tm, d), BF16), pltpu.VMEM((tm, D_FF), BF16),
                        pltpu.SemaphoreType.DMA((2,)), pltpu.SemaphoreType.DMA((2,))],
    )
    return pl.pallas_call(
        kern,
        grid_spec=grid_spec,
        out_shape=jax.ShapeDtypeStruct((t * TOP_K + 2 * tm, d), F32),
        compiler_params=_cparams(1),
        name="expert_ffn",
    )(tile_expert, n_used, row_tok, row_tok, row_dst, x1, wg, wu, wd, bg, bu, bd)


def _row_table_kernel(first_ref, nvalid_ref, a_ref, tok_ref, dst_ref, *, tm, t, n_tiles):
    n_assign = TOP_K * t
    rows_per_tile = tm // LANES
    lane = lax.broadcasted_iota(I32, (rows_per_tile, LANES), 1)
    pos = lax.broadcasted_iota(I32, (rows_per_tile, LANES), 0) * LANES + lane

    def body(i, carry):
        first = first_ref[i]
        q = lax.shift_right_logical(first, 7)
        m = first & (LANES - 1)
        rows = a_ref[pl.ds(q, rows_per_tile + 1), :]
        rolled = pltpu.roll(rows, lax.rem(LANES - m, LANES), 1)
        a = jnp.where(lane + m < LANES, rolled[:rows_per_tile], rolled[1:])
        valid = pos < nvalid_ref[i]
        tok = a
        for k in range(1, TOP_K):
            tok = tok - jnp.where(a >= k * t, t, 0)
        out = pl.ds(i * rows_per_tile, rows_per_tile)
        tok_ref[out, :] = jnp.where(valid, tok, 0)
        dst_ref[out, :] = jnp.where(valid, a, n_assign + lax.rem(i, 2) * tm + pos)
        return carry

    lax.fori_loop(0, n_tiles, body, 0)


def _route(top_idx, count_parts, tm):
    t = top_idx.shape[1]
    n_assign = t * TOP_K
    n_tiles = n_assign // tm + N_EXPERTS
    e_flat = top_idx.reshape(-1)
    ids = jnp.arange(n_assign, dtype=I32)
    _, a_sorted = lax.sort((e_flat, ids), num_keys=1, is_stable=True)
    counts = jnp.sum(count_parts, axis=1)
    padded = ((counts + tm - 1) // tm) * tm
    start = jnp.cumsum(counts) - counts
    pend = jnp.cumsum(padded)
    pstart = pend - padded
    n_used = (pend[-1] // tm).astype(I32)
    tile = jnp.arange(n_tiles, dtype=I32)
    te = jnp.sum((tile[:, None] * tm >= pend[None, :]).astype(I32), axis=1)
    last = jnp.sum(((n_used - 1) * tm >= pend).astype(I32))
    te = jnp.minimum(te, last)
    in_expert = tile * tm - pstart[te]
    n_valid = jnp.where(tile < n_used, jnp.clip(counts[te] - in_expert, 0, tm), 0).astype(I32)
    first = jnp.where(tile < n_used, start[te] + in_expert, 0).astype(I32)

    a2 = jnp.pad(a_sorted, (0, 8 * LANES)).reshape(-1, LANES)
    rows_out = n_tiles * tm // LANES
    kern = functools.partial(_row_table_kernel, tm=tm, t=t, n_tiles=n_tiles)
    table = jax.ShapeDtypeStruct((rows_out, LANES), I32)
    row_tok, row_dst = pl.pallas_call(
        kern,
        grid_spec=pltpu.PrefetchScalarGridSpec(
            num_scalar_prefetch=2,
            grid=(1,),
            in_specs=[pl.BlockSpec(a2.shape, lambda i, f, nv: (0, 0))],
            out_specs=[pl.BlockSpec((rows_out, LANES), lambda i, f, nv: (0, 0))] * 2,
        ),
        out_shape=[table, table],
        compiler_params=_cparams(1),
        name="row_tables",
    )(first, n_valid, a2)
    return (te, n_used.reshape(1), row_tok.reshape(n_tiles, 1, tm), row_dst.reshape(n_tiles, 1, tm))


def _combine_kernel(y0_ref, y1_ref, y2_ref, y3_ref, x_ref, gate_ref, g_ref, b_ref, o_ref, *, alpha):
    tm = x_ref.shape[0]
    cols = []
    for p in range(tm // LANES):
        piece = jnp.concatenate([gate_ref[:, p * LANES:(p + 1) * LANES],
                                 jnp.zeros((LANES - gate_ref.shape[0], LANES), F32)], axis=0)
        cols.append(jnp.transpose(piece))
    gates = [jnp.concatenate([c[:, k:k + 1] for c in cols], axis=0) for k in range(TOP_K)]
    ffn = ((gates[0] * y0_ref[...] + gates[1] * y1_ref[...])
           + (gates[2] * y2_ref[...] + gates[3] * y3_ref[...]))
    o_ref[...] = _layer_norm(alpha * x_ref[...] + ffn, g_ref[...], b_ref[...])


def _combine_ln(y4, x1, gates, g, b, alpha):
    t, d = x1.shape
    tm = min(ROW_TILE, t)
    nb = t // tm
    kern = functools.partial(_combine_kernel, alpha=alpha)
    slot_specs = [pl.BlockSpec((tm, d), functools.partial(lambda i, k: (k * nb + i, 0), k=k))
                  for k in range(TOP_K)]
    return pl.pallas_call(
        kern,
        grid=(nb,),
        in_specs=slot_specs + [pl.BlockSpec((tm, d), lambda i: (i, 0)),
                               pl.BlockSpec((gates.shape[0], tm), lambda i: (0, i)),
                               pl.BlockSpec((1, d), lambda i: (0, 0)),
                               pl.BlockSpec((1, d), lambda i: (0, 0))],
        out_specs=pl.BlockSpec((tm, d), lambda i: (i, 0)),
        out_shape=jax.ShapeDtypeStruct((t, d), F32),
        compiler_params=_cparams(1),
        name="combine_ln",
    )(y4, y4, y4, y4, x1, gates, g, b)


def _deinterleave_kernel(w_ref, g_ref, u_ref):
    half = MXU_TILE // 2
    src = lax.broadcasted_iota(I32, (MXU_TILE, MXU_TILE), 0)
    dst = lax.broadcasted_iota(I32, (MXU_TILE, MXU_TILE), 1)
    perm = jnp.where(src == jnp.where(dst < half, 2 * dst, 2 * (dst - half) + 1), 1.0, 0.0).astype(BF16)
    for c in range(w_ref.shape[1] // MXU_TILE):
        blk = w_ref[:, c * MXU_TILE:(c + 1) * MXU_TILE].astype(BF16)
        sp = jnp.dot(blk, perm, preferred_element_type=F32)
        g_ref[:, c * half:(c + 1) * half] = sp[:, :half].astype(BF16)
        u_ref[:, c * half:(c + 1) * half] = sp[:, half:].astype(BF16)


def _deinterleave(w_gu):
    n, d, f2 = w_gu.shape
    tr = min(1024, d)
    out = jax.ShapeDtypeStruct((n, d, f2 // 2), BF16)
    return pl.pallas_call(
        _deinterleave_kernel,
        grid=(n, d // tr),
        in_specs=[pl.BlockSpec((None, tr, f2), lambda e, r: (e, r, 0))],
        out_specs=[pl.BlockSpec((None, tr, f2 // 2), lambda e, r: (e, r, 0))] * 2,
        out_shape=[out, out],
        compiler_params=_cparams(2),
        name="split_gate_up",
    )(w_gu)


def _rotary_tables(seq):
    half = RET_HEAD_DIM // 2
    inv_freq = ROPE_BASE ** (-jnp.linspace(0.0, 1.0, half, dtype=F32))
    ang = jnp.arange(seq).astype(F32)[:, None] * inv_freq[None, :]
    cos, sin = jnp.cos(ang), jnp.sin(ang)
    return (jnp.concatenate([cos, cos], axis=-1), jnp.concatenate([-sin, sin], axis=-1))


def _decay_tables():
    c = RET_CHUNK
    lg = jnp.log1p(-jnp.exp2(-5.0 - jnp.arange(RET_HEADS, dtype=F32)))
    i = jnp.arange(c, dtype=F32)
    diff = i[:, None] - i[None, :]
    intra = jnp.where(diff[None] >= 0, jnp.exp(jnp.maximum(diff, 0.0)[None] * lg[:, None, None]), 0.0)
    q_dec = jnp.exp((i + 1.0)[None, :] * lg[:, None])[..., None]
    k_dec = jnp.exp((c - 1.0 - i)[None, :] * lg[:, None])[..., None]
    c_dec = jnp.exp(c * lg)[:, None, None]
    full = (RET_HEADS, c, c)
    return jnp.stack([intra.astype(F32), jnp.broadcast_to(q_dec, full),
                      jnp.broadcast_to(k_dec, full), jnp.broadcast_to(c_dec, full)])


def _split_w_in(w_in):
    pts = np.cumsum((0,) + IN_SIZES)
    a = w_in[..., pts[0]:pts[4]]
    idx = w_in[..., pts[4]:pts[6]]
    idx = jnp.pad(idx, ((0, 0), (0, 0), (0, LANES - idx.shape[-1])))
    rest = jnp.concatenate([w_in[..., pts[6]:], idx], axis=-1)
    return a.astype(BF16), rest.astype(BF16)


def _cast_kernel(x_ref, o_ref):
    o_ref[...] = x_ref[...].astype(o_ref.dtype)


def _to_bf16(w):
    n, r, c = w.shape
    blk = (2 if n % 2 == 0 else 1, r, c)
    return pl.pallas_call(
        _cast_kernel,
        grid=(n // blk[0],),
        in_specs=[pl.BlockSpec(blk, lambda i: (i, 0, 0))],
        out_specs=pl.BlockSpec(blk, lambda i: (i, 0, 0)),
        out_shape=jax.ShapeDtypeStruct(w.shape, BF16),
        compiler_params=_cparams(1),
        name="cast_bf16",
    )(w)


def _v_transposed(a_pack, batch, seq):
    v = a_pack[:, ATTN_W + KV_W:ATTN_W + 2 * KV_W].reshape(batch, seq, ATTN_KV_HEADS, HEAD_DIM)
    vt = jnp.transpose(v, (0, 2, 3, 1))
    ones = jnp.ones((batch, ATTN_KV_HEADS, BF16_ROWS, seq), BF16)
    return jnp.concatenate([vt, ones], axis=2)


def kernel(x, w_in, conv_w, w_out, ln1_g, ln1_b, router_w, router_b,
           w_gu, b_gu, w_down, b_down, ln2_g, ln2_b):
    batch, seq, d = x.shape
    depth = w_in.shape[0]
    t = batch * seq
    alpha = (2 * depth) ** 0.25
    topk = min(IDX_TOPK_MAX, seq // 4)
    tm_e = EXPERT_TILE

    wa, wr = _split_w_in(w_in)
    idx_blk = (wr.shape[-1] - LANES) // LANES
    w_out_b = w_out.astype(BF16)
    rw_t = jnp.swapaxes(router_w, 1, 2).astype(BF16)
    n_exp = depth * N_EXPERTS
    wg, wu = _deinterleave(w_gu.reshape(n_exp, d, 2 * D_FF))
    wd = _to_bf16(w_down.reshape(n_exp, D_FF, d))
    bg = b_gu[..., 0::2].reshape(n_exp, 1, D_FF)
    bu = b_gu[..., 1::2].reshape(n_exp, 1, D_FF)
    bd = b_down.reshape(n_exp, 1, d)
    cw = jnp.pad(conv_w, ((0, 0), (0, 8 - CONV_WIDTH), (0, 0)))
    cos_t, sin_t = _rotary_tables(seq)
    dec = _decay_tables()

    xf = x.reshape(t, d)
    for l in range(depth):
        a_pack = _matmul(xf, wa[l], min(512, t), ATTN_W, BF16, "in_proj_attn",
                         first_block_scale=HEAD_DIM ** -0.5 * LOG2_E)
        r_pack = _matmul(xf, wr[l], min(ROW_TILE, t), wr.shape[-1], F32, "in_proj_rest")
        attn = _attention(a_pack, r_pack, idx_blk, _v_transposed(a_pack, batch, seq), batch, seq, topk)
        conv, ret = _retconv(r_pack, cw[l], cos_t, sin_t, dec, batch, seq)
        x1, top_idx, gates, count_parts = _outproj_ln_router(
            attn, conv, ret, xf, w_out_b[l], ln1_g[l][None], ln1_b[l][None],
            rw_t[l], router_b[l][:, None], alpha)
        te, n_used, row_tok, row_dst = _route(top_idx, count_parts, tm_e)
        y4 = _experts(x1, te, n_used, row_tok, row_dst, wg, wu, wd, bg, bu, bd, l, tm_e)
        xf = _combine_ln(y4, x1, gates, ln2_g[l][None], ln2_b[l][None], alpha)
    return xf.reshape(batch, seq, d)
```

```python
import functools

import jax
import jax.numpy as jnp
import numpy as np
from jax import lax
from jax.experimental import pallas as pl
from jax.experimental.pallas import tpu as pltpu

F32 = jnp.float32
BF16 = jnp.bfloat16
I32 = jnp.int32

ATTN_HEADS = 16
ATTN_KV_HEADS = 4
HEAD_DIM = 64
IDX_HEADS = 8
IDX_DIM = 64
IDX_TOPK_MAX = 256
Q_BLOCK = 128
CONV_DIM = 512
CONV_WIDTH = 3
RET_HEADS = 4
RET_HEAD_DIM = 128
RET_CHUNK = 128
ROPE_BASE = 10000.0
N_EXPERTS = 32
TOP_K = 4
D_FF = 768
SWIGLU_LIMIT = 7.0
SWIGLU_ALPHA = 1.702
LN_EPS = 1e-5

ATTN_W = ATTN_HEADS * HEAD_DIM
KV_W = ATTN_KV_HEADS * HEAD_DIM
IDXQ_W = IDX_HEADS * IDX_DIM
RET_W = RET_HEADS * RET_HEAD_DIM
IN_SIZES = (ATTN_W, KV_W, KV_W, IDXQ_W, IDX_DIM, IDX_HEADS,
            CONV_DIM, CONV_DIM, CONV_DIM, RET_W, RET_W, RET_W, RET_W)

LANES = 128
MXU_TILE = 256
BF16_ROWS = 16
KEY_CHUNK = 512
SEQ_TILE = 512
ROW_TILE = 256
EXPERT_TILE = 256
ISSUE_UNROLL = 8
VMEM_LIMIT = 56 * 1024 * 1024

NEG_BIG = -1e30
LOG2_E = 1.4426950408889634
V_ROWS = HEAD_DIM + BF16_ROWS
BISECT_STEPS = 20
BISECT_EXTRA = 4
BISECT_MAX_ROUNDS = 80


def _cparams(n_axes):
    return pltpu.CompilerParams(dimension_semantics=("arbitrary",) * n_axes,
                                vmem_limit_bytes=VMEM_LIMIT)


def _mm_kernel(x_ref, w_ref, o_ref, *, first_block_scale):
    acc = jnp.dot(x_ref[...].astype(BF16), w_ref[...], preferred_element_type=F32)
    if first_block_scale is not None:
        acc = acc * jnp.where(pl.program_id(0) == 0, first_block_scale, 1.0)
    o_ref[...] = acc.astype(o_ref.dtype)


def _matmul(x, w, tm, tn, out_dtype, name, first_block_scale=None):
    m, k = x.shape
    n = w.shape[1]
    return pl.pallas_call(
        functools.partial(_mm_kernel, first_block_scale=first_block_scale),
        grid=(n // tn, m // tm),
        in_specs=[pl.BlockSpec((tm, k), lambda j, i: (i, 0)),
                  pl.BlockSpec((k, tn), lambda j, i: (0, j))],
        out_specs=pl.BlockSpec((tm, tn), lambda j, i: (i, j)),
        out_shape=jax.ShapeDtypeStruct((m, n), out_dtype),
        compiler_params=_cparams(2),
        name=name,
    )(x, w)


def _attn_kernel(q_ref, qi_ref, k_ref, vt_ref, ki_ref, iw_ref, o_ref,
                 qs_ref, qis_ref, s_ref, bias_ref, acc_ref, ot_ref, *, topk, kc, halves):
    i = pl.program_id(1)
    q0 = i * Q_BLOCK
    nch = (q0 + Q_BLOCK + kc - 1) // kc
    kc2 = halves * kc
    nch2 = (nch + halves - 1) // halves
    nt = (((1,), (1,)), ((), ()))

    for h in range(ATTN_HEADS):
        qs_ref[h * Q_BLOCK:(h + 1) * Q_BLOCK, :] = q_ref[:, h * HEAD_DIM:(h + 1) * HEAD_DIM]
    for h in range(IDX_HEADS):
        qis_ref[h * Q_BLOCK:(h + 1) * Q_BLOCK, :] = qi_ref[:, h * IDX_DIM:(h + 1) * IDX_DIM]

    wt = jnp.transpose(iw_ref[...])[IDX_DIM:IDX_DIM + IDX_HEADS, :] * (IDXQ_W ** -0.5)
    qpos = q0 + lax.broadcasted_iota(I32, (kc, LANES), 1)
    krow = lax.broadcasted_iota(I32, (kc, LANES), 0)

    def score_chunk(c, carry):
        lo, hi = carry
        for half in range(halves):
            r0 = pl.multiple_of(c * kc2 + half * kc, kc)
            ki_c = ki_ref[pl.ds(r0, kc), :][:, :IDX_DIM].astype(BF16)
            r = lax.dot_general(ki_c, qis_ref[...], nt, preferred_element_type=F32)
            sc = jnp.zeros((kc, LANES), F32)
            for h in range(IDX_HEADS):
                sc = sc + jnp.maximum(r[:, h * LANES:(h + 1) * LANES], 0.0) * wt[h:h + 1, :]
            causal = r0 + krow <= qpos
            s_ref[pl.ds(r0, kc), :] = jnp.where(causal, sc, -jnp.inf)
            lo = jnp.minimum(lo, jnp.min(jnp.where(causal, sc, jnp.inf), axis=0, keepdims=True))
            hi = jnp.maximum(hi, jnp.max(jnp.where(causal, sc, -jnp.inf), axis=0, keepdims=True))
        return lo, hi

    lo0, hi0 = lax.fori_loop(0, nch2, score_chunk,
                             (jnp.full((1, LANES), jnp.inf, F32), jnp.full((1, LANES), -jnp.inf, F32)))

    def column_sum(vals):
        acc = lax.fori_loop(0, nch, lambda c, a: a + vals(s_ref[pl.ds(pl.multiple_of(c * kc, kc), kc), :]),
                            jnp.zeros((8, LANES), I32))
        return jnp.sum(acc, axis=0, keepdims=True)

    def count(pred):
        return column_sum(lambda s: jnp.sum(jnp.where(pred(s), 1, 0).reshape(kc // 8, 8, LANES), axis=0))

    def bisect(_, carry):
        lo, hi = carry
        mid = lo + (hi - lo) * 0.5
        mid = jnp.where(mid == lo, hi, mid)
        ge = count(lambda s: s >= mid) >= topk
        return jnp.where(ge, mid, lo), jnp.where(ge, hi, mid)

    def settle(lo):
        def body(c, m):
            s = s_ref[pl.ds(pl.multiple_of(c * kc, kc), kc), :]
            return jnp.minimum(m, jnp.min(jnp.where(s >= lo, s, jnp.inf).reshape(kc // 8, 8, LANES), axis=0))
        m = lax.fori_loop(0, nch, body, jnp.full((8, LANES), jnp.inf, F32))
        thr = jnp.min(m, axis=0, keepdims=True)
        return thr, count(lambda s: s > thr)

    lo, hi = lax.fori_loop(0, BISECT_STEPS, bisect, (lo0, hi0))
    thr, cnt_gt = settle(lo)

    def unsettled(state):
        _, _, _, cnt_gt, rounds = state
        return (jnp.max(cnt_gt) >= topk) & (rounds < BISECT_MAX_ROUNDS)

    def refine(state):
        lo, hi, _, _, rounds = state
        lo, hi = lax.fori_loop(0, BISECT_EXTRA, bisect, (lo, hi))
        thr, cnt_gt = settle(lo)
        return lo, hi, thr, cnt_gt, rounds + 1

    _, _, thr, cnt_gt, _ = lax.while_loop(unsettled, refine, (lo, hi, thr, cnt_gt, jnp.int32(0)))
    cnt_eq = count(lambda s: s == thr)
    need = topk - cnt_gt
    has_ties = jnp.max(jnp.where(cnt_eq > need, 1, 0)) > 0

    @pl.when(jnp.logical_not(has_ties))
    def _():
        def body(c, carry):
            r0 = pl.multiple_of(c * kc, kc)
            s = s_ref[pl.ds(r0, kc), :]
            bias_ref[pl.ds(r0, kc), :] = jnp.where(s >= thr, 0.0, NEG_BIG)
            return carry
        lax.fori_loop(0, nch, body, 0)

    @pl.when(has_ties)
    def _():
        tri = (lax.broadcasted_iota(I32, (kc, kc), 1)
               < lax.broadcasted_iota(I32, (kc, kc), 0)).astype(BF16)
        need_f = need.astype(F32)

        def body(c, seen):
            r0 = pl.multiple_of(c * kc, kc)
            s = s_ref[pl.ds(r0, kc), :]
            eq = s == thr
            eq_f = jnp.where(eq, 1.0, 0.0)
            before = jnp.dot(tri, eq_f.astype(BF16), preferred_element_type=F32) + seen
            sel = (s > thr) | (eq & (before < need_f))
            bias_ref[pl.ds(r0, kc), :] = jnp.where(sel, 0.0, NEG_BIG)
            return seen + jnp.sum(eq_f, axis=0, keepdims=True)
        lax.fori_loop(0, nch, body, jnp.zeros((1, LANES), F32))

    acc_ref[...] = jnp.zeros_like(acc_ref)
    rep = ATTN_HEADS // ATTN_KV_HEADS

    def attn_chunk(c, ms):
        r0 = pl.multiple_of(c * kc, kc)
        bias = bias_ref[pl.ds(r0, kc), :]
        k_c = k_ref[pl.ds(r0, kc), :]
        new_ms = []
        sts = []
        for g in range(ATTN_KV_HEADS):
            k_g = k_c[:, g * HEAD_DIM:(g + 1) * HEAD_DIM]
            qs_g = qs_ref[g * rep * Q_BLOCK:(g + 1) * rep * Q_BLOCK, :]
            sts.append(lax.dot_general(k_g, qs_g, nt, preferred_element_type=F32))
        for g in range(ATTN_KV_HEADS):
            vt_g = vt_ref[g, :, pl.ds(r0, kc)]
            for r in range(rep):
                h = g * rep + r
                s = sts[g][:, r * LANES:(r + 1) * LANES] + bias
                m_new = jnp.maximum(ms[h], jnp.max(s, axis=0, keepdims=True))
                alpha = jnp.exp2(ms[h] - m_new)
                p = jnp.exp2(s - m_new).astype(BF16)
                new_ms.append(m_new)
                pv = jnp.dot(vt_g, p, preferred_element_type=F32)
                rows = slice(h * V_ROWS, (h + 1) * V_ROWS)
                acc_ref[rows, :] = alpha * acc_ref[rows, :] + pv
        return tuple(new_ms)

    init = tuple(jnp.full((1, LANES), NEG_BIG, F32) for _ in range(ATTN_HEADS))
    lax.fori_loop(0, nch, attn_chunk, init)

    for h in range(ATTN_HEADS):
        num = acc_ref[h * V_ROWS:h * V_ROWS + HEAD_DIM, :]
        den = acc_ref[h * V_ROWS + HEAD_DIM:h * V_ROWS + HEAD_DIM + 1, :]
        ot_ref[h * HEAD_DIM:(h + 1) * HEAD_DIM, :] = num / den
    o_ref[...] = jnp.transpose(ot_ref[...]).astype(o_ref.dtype)


def _attention(a_pack, i_pack, idx_blk, vt, batch, seq, topk):
    nqb = seq // Q_BLOCK
    kc = min(KEY_CHUNK, seq)
    kern = functools.partial(_attn_kernel, topk=topk, kc=kc, halves=2 if seq % (2 * kc) == 0 else 1)
    return pl.pallas_call(
        kern,
        grid=(batch, nqb),
        in_specs=[
            pl.BlockSpec((Q_BLOCK, ATTN_W), lambda b, i: (b * nqb + i, 0)),
            pl.BlockSpec((Q_BLOCK, IDXQ_W), lambda b, i: (b * nqb + i, (ATTN_W + 2 * KV_W) // IDXQ_W)),
            pl.BlockSpec((seq, KV_W), lambda b, i: (b, ATTN_W // KV_W)),
            pl.BlockSpec((None, ATTN_KV_HEADS, V_ROWS, seq), lambda b, i: (b, 0, 0, 0)),
            pl.BlockSpec((seq, LANES), lambda b, i: (b, idx_blk)),
            pl.BlockSpec((Q_BLOCK, LANES), lambda b, i: (b * nqb + i, idx_blk)),
        ],
        out_specs=pl.BlockSpec((Q_BLOCK, ATTN_W), lambda b, i: (b * nqb + i, 0)),
        out_shape=jax.ShapeDtypeStruct((batch * seq, ATTN_W), BF16),
        scratch_shapes=[
            pltpu.VMEM((ATTN_HEADS * Q_BLOCK, HEAD_DIM), BF16),
            pltpu.VMEM((IDX_HEADS * Q_BLOCK, IDX_DIM), BF16),
            pltpu.VMEM((seq + kc, LANES), F32),
            pltpu.VMEM((seq, LANES), F32),
            pltpu.VMEM((ATTN_HEADS * V_ROWS, LANES), F32),
            pltpu.VMEM((ATTN_W, LANES), F32),
        ],
        compiler_params=_cparams(2),
        name="dsa_attention",
    )(a_pack, a_pack, a_pack, vt, i_pack, i_pack)


def _retconv_kernel(cb_ref, cc_ref, ch_ref, rq_ref, rk_ref, rv_ref, rg_ref,
                    cw_ref, cos_ref, sin_ref, dec_ref, conv_o, ret_o,
                    halo_ref, state_ref, *, ts):
    j = pl.program_id(1)

    @pl.when(j == 0)
    def _():
        halo_ref[...] = jnp.zeros_like(halo_ref)
        state_ref[...] = jnp.zeros_like(state_ref)

    u = cc_ref[...] * ch_ref[...]
    row = lax.broadcasted_iota(I32, u.shape, 0)
    prev1 = halo_ref[7:8, :]
    prev2 = halo_ref[6:7, :]
    u1 = jnp.where(row == 0, prev1, pltpu.roll(u, 1, 0))
    u2 = jnp.where(row == 0, prev2, jnp.where(row == 1, prev1, pltpu.roll(u, 2, 0)))
    y = cw_ref[0:1, :] * u2 + cw_ref[1:2, :] * u1 + cw_ref[2:3, :] * u
    conv_o[...] = (cb_ref[...] * y).astype(conv_o.dtype)
    halo_ref[...] = u[ts - 8:ts, :]

    cos = cos_ref[...]
    sin = sin_ref[...]
    half = RET_HEAD_DIM // 2
    nt = (((1,), (1,)), ((), ()))
    tn = (((0,), (0,)), ((), ()))
    for h in range(RET_HEADS):
        cols = slice(h * RET_HEAD_DIM, (h + 1) * RET_HEAD_DIM)
        qh = rq_ref[:, cols]
        kh = rk_ref[:, cols]
        qh = qh * cos + pltpu.roll(qh, half, 1) * sin
        kh = (kh * cos + pltpu.roll(kh, half, 1) * sin) * (RET_HEAD_DIM ** -0.5)
        vh = rv_ref[:, cols]
        intra = dec_ref[0, h]
        q_dec = dec_ref[1, h]
        k_dec = dec_ref[2, h]
        c_dec = dec_ref[3, h]
        state = state_ref[h]
        outs = []
        for c in range(ts // RET_CHUNK):
            rows = slice(c * RET_CHUNK, (c + 1) * RET_CHUNK)
            qc, kc_, vc = qh[rows], kh[rows], vh[rows].astype(BF16)
            s = lax.dot_general(qc.astype(BF16), kc_.astype(BF16), nt,
                                preferred_element_type=F32) * intra
            o = (jnp.dot(s.astype(BF16), vc, preferred_element_type=F32)
                 + jnp.dot((qc * q_dec).astype(BF16), state.astype(BF16),
                           preferred_element_type=F32))
            state = state * c_dec + lax.dot_general((kc_ * k_dec).astype(BF16), vc, tn,
                                                    preferred_element_type=F32)
            outs.append(o)
        state_ref[h] = state
        o = jnp.concatenate(outs, axis=0)
        mu = jnp.mean(o, axis=-1, keepdims=True)
        d = o - mu
        var = jnp.mean(d * d, axis=-1, keepdims=True)
        on = d * lax.rsqrt(var + LN_EPS)
        g = rg_ref[:, cols]
        ret_o[:, cols] = (on * (g / (1.0 + jnp.exp(-g)))).astype(ret_o.dtype)


def _retconv(r_pack, conv_w, cos_t, sin_t, dec, batch, seq):
    ts = min(SEQ_TILE, seq)
    nst = seq // ts
    t = batch * seq
    kern = functools.partial(_retconv_kernel, ts=ts)
    specs = [pl.BlockSpec((ts, CONV_DIM),
                          functools.partial(lambda b, j, n: (b * nst + j, n), n=n))
             for n in range(7)]
    return pl.pallas_call(
        kern,
        grid=(batch, nst),
        in_specs=specs + [
            pl.BlockSpec((8, CONV_DIM), lambda b, j: (0, 0)),
            pl.BlockSpec((ts, RET_HEAD_DIM), lambda b, j: (j, 0)),
            pl.BlockSpec((ts, RET_HEAD_DIM), lambda b, j: (j, 0)),
            pl.BlockSpec((4, RET_HEADS, RET_CHUNK, RET_CHUNK), lambda b, j: (0, 0, 0, 0)),
        ],
        out_specs=[pl.BlockSpec((ts, CONV_DIM), lambda b, j: (b * nst + j, 0)),
                   pl.BlockSpec((ts, RET_W), lambda b, j: (b * nst + j, 0))],
        out_shape=[jax.ShapeDtypeStruct((t, CONV_DIM), BF16),
                   jax.ShapeDtypeStruct((t, RET_W), BF16)],
        scratch_shapes=[pltpu.VMEM((8, CONV_DIM), F32),
                        pltpu.VMEM((RET_HEADS, RET_HEAD_DIM, RET_HEAD_DIM), F32)],
        compiler_params=_cparams(2),
        name="conv_retention",
    )(*([r_pack] * 7), conv_w, cos_t, sin_t, dec)


def _layer_norm(y, g, b):
    mu = jnp.mean(y, axis=-1, keepdims=True)
    d = y - mu
    var = jnp.mean(d * d, axis=-1, keepdims=True)
    return d * lax.rsqrt(var + LN_EPS) * g + b


def _outproj_kernel(attn_ref, conv_ref, ret_ref, x_ref, w_ref, g_ref, b_ref, rw_ref, rb_ref,
                    x1_ref, idx_ref, gate_ref, cnt_ref, *, alpha):
    mix = jnp.dot(attn_ref[...], w_ref[0:ATTN_W, :], preferred_element_type=F32)
    mix = mix + jnp.dot(conv_ref[...], w_ref[ATTN_W:ATTN_W + CONV_DIM, :],
                        preferred_element_type=F32)
    mix = mix + jnp.dot(ret_ref[...], w_ref[ATTN_W + CONV_DIM:, :], preferred_element_type=F32)
    x1 = _layer_norm(alpha * x_ref[...] + mix, g_ref[...], b_ref[...])
    x1_ref[...] = x1

    nt = (((1,), (1,)), ((), ()))
    logits = lax.dot_general(rw_ref[...], x1.astype(BF16), nt,
                             preferred_element_type=F32) + rb_ref[...]
    eidx = lax.broadcasted_iota(I32, logits.shape, 0)
    vals, idxs = [], []
    chosen = jnp.zeros(logits.shape, I32)
    for _ in range(TOP_K):
        mx = jnp.max(logits, axis=0, keepdims=True)
        am = jnp.min(jnp.where(logits == mx, eidx, N_EXPERTS), axis=0, keepdims=True)
        vals.append(mx)
        idxs.append(am)
        chosen = chosen + jnp.where(eidx == am, 1, 0)
        logits = jnp.where(eidx == am, -jnp.inf, logits)

    @pl.when(pl.program_id(0) == 0)
    def _():
        cnt_ref[...] = jnp.zeros_like(cnt_ref)
    part = chosen[:, 0:LANES]
    for p in range(1, chosen.shape[1] // LANES):
        part = part + chosen[:, p * LANES:(p + 1) * LANES]
    cnt_ref[...] += part
    v = jnp.concatenate(vals, axis=0)
    e = jnp.exp(v - v[0:1, :])
    gates = e / jnp.sum(e, axis=0, keepdims=True)
    gate_ref[...] = jnp.concatenate([gates, jnp.zeros_like(gates)], axis=0)
    idx_ref[...] = jnp.concatenate(idxs, axis=0)


def _outproj_ln_router(attn, conv, ret, x, w_out, g, b, rw_t, rb, alpha):
    t, d = x.shape
    tm = min(ROW_TILE, t)
    kern = functools.partial(_outproj_kernel, alpha=alpha)
    row = lambda w: pl.BlockSpec((tm, w), lambda i: (i, 0))
    full = lambda s: pl.BlockSpec(s, lambda i: (0,) * len(s))
    return pl.pallas_call(
        kern,
        grid=(t // tm,),
        in_specs=[row(ATTN_W), row(CONV_DIM), row(RET_W), row(d), full((d, d)),
                  full((1, d)), full((1, d)), full((N_EXPERTS, d)), full((N_EXPERTS, 1))],
        out_specs=[row(d),
                   pl.BlockSpec((TOP_K, tm), lambda i: (0, i)),
                   pl.BlockSpec((2 * TOP_K, tm), lambda i: (0, i)),
                   pl.BlockSpec((N_EXPERTS, LANES), lambda i: (0, 0))],
        out_shape=[jax.ShapeDtypeStruct((t, d), F32),
                   jax.ShapeDtypeStruct((TOP_K, t), I32),
                   jax.ShapeDtypeStruct((2 * TOP_K, t), F32),
                   jax.ShapeDtypeStruct((N_EXPERTS, LANES), I32)],
        compiler_params=_cparams(1),
        name="outproj_ln_router",
    )(attn, conv, ret, x, w_out, g, b, rw_t, rb)


def _expert_kernel(te_ref, nu_ref, tokc_ref, tokn_ref, dst_ref, x_hbm,
                   wg_ref, wu_ref, wd_ref, bg_ref, bu_ref, bd_ref, out_hbm,
                   xbuf, ybuf, xb_ref, act_ref, gsem, ssem, *, tm):
    i = pl.program_id(0)
    n_used = nu_ref[0]
    slot = lax.rem(i, 2)
    d = xbuf.shape[2]
    n_real = out_hbm.shape[0] - 2 * tm

    def start_gather(tok_ref, s):
        def body(r, carry):
            pltpu.make_async_copy(x_hbm.at[pl.ds(tok_ref[0, 0, r], 1)],
                                  xbuf.at[s, pl.ds(r, 1)], gsem.at[s]).start()
            return carry
        lax.fori_loop(0, tm, body, 0, unroll=8)

    def wait_gather(s):
        pltpu.make_async_copy(x_hbm.at[pl.ds(0, tm)], xbuf.at[s], gsem.at[s]).wait()

    def wait_scatter(s):
        pltpu.make_async_copy(ybuf.at[s], out_hbm.at[pl.ds(0, tm)], ssem.at[s]).wait()

    @pl.when(i == 0)
    def _():
        ybuf[0] = jnp.zeros(ybuf.shape[1:], F32)
        for s in range(2):
            pltpu.make_async_copy(ybuf.at[0], out_hbm.at[pl.ds(n_real + s * tm, tm)],
                                  ssem.at[0]).start()
        for s in range(2):
            pltpu.make_async_copy(ybuf.at[0], out_hbm.at[pl.ds(n_real + s * tm, tm)],
                                  ssem.at[0]).wait()

    @pl.when((i == 0) & (n_used > 0))
    def _():
        start_gather(tokc_ref, 0)

    @pl.when(i + 1 < n_used)
    def _():
        start_gather(tokn_ref, 1 - slot)

    @pl.when(i < n_used)
    def _():
        wait_gather(slot)
        xb_ref[...] = xbuf[slot].astype(BF16)
        for c in range(D_FF // MXU_TILE):
            cols = slice(c * MXU_TILE, (c + 1) * MXU_TILE)
            gt = jnp.dot(xb_ref[...], wg_ref[:, cols], preferred_element_type=F32) + bg_ref[:, cols]
            up = jnp.dot(xb_ref[...], wu_ref[:, cols], preferred_element_type=F32) + bu_ref[:, cols]
            gt = jnp.minimum(gt, SWIGLU_LIMIT)
            up = jnp.clip(up, -SWIGLU_LIMIT, SWIGLU_LIMIT)
            act_ref[:, cols] = ((up + 1.0) * (gt / (1.0 + jnp.exp(-SWIGLU_ALPHA * gt)))).astype(BF16)
        for c in range(d // MXU_TILE):
            cols = slice(c * MXU_TILE, (c + 1) * MXU_TILE)
            ybuf[slot, :, cols] = (jnp.dot(act_ref[...], wd_ref[:, cols], preferred_element_type=F32)
                                   + bd_ref[:, cols])

        def body(j, carry):
            for u in range(ISSUE_UNROLL):
                r = j * ISSUE_UNROLL + u
                pltpu.make_async_copy(ybuf.at[slot, pl.ds(r, 1)],
                                      out_hbm.at[pl.ds(dst_ref[0, 0, r], 1)],
                                      ssem.at[slot]).start(priority=u % 2)
            return carry
        lax.fori_loop(0, tm // ISSUE_UNROLL, body, 0)

        @pl.when(i >= 1)
        def _():
            wait_scatter(1 - slot)

        @pl.when(i == n_used - 1)
        def _():
            wait_scatter(slot)


def _experts(x1, tile_expert, n_used, row_tok, row_dst, wg, wu, wd, bg, bu, bd, layer, tm):
    t, d = x1.shape
    n_tiles = tile_expert.shape[0]
    kern = functools.partial(_expert_kernel, tm=tm)
    smem_blk = lambda f: pl.BlockSpec((1, 1, tm), f, memory_space=pltpu.SMEM)
    wspec = lambda s: pl.BlockSpec((None,) + s, lambda i, te, nu: (layer * N_EXPERTS + te[i], 0, 0))
    grid_spec = pltpu.PrefetchScalarGridSpec(
        num_scalar_prefetch=2,
        grid=(n_tiles,),
        in_specs=[
            smem_blk(lambda i, te, nu: (i, 0, 0)),
            smem_blk(lambda i, te, nu: (jnp.minimum(i + 1, n_tiles - 1), 0, 0)),
            smem_blk(lambda i, te, nu: (i, 0, 0)),
            pl.BlockSpec(memory_space=pl.ANY),
            wspec((d, D_FF)), wspec((d, D_FF)), wspec((D_FF, d)),
            wspec((1, D_FF)), wspec((1, D_FF)), wspec((1, d)),
        ],
        out_specs=pl.BlockSpec(memory_space=pl.ANY),
        scratch_shapes=[pltpu.VMEM((2, tm, d), F32), pltpu.VMEM((2, tm, d), F32),
                        pltpu.VMEM((tm, d), BF16), pltpu.VMEM((tm, D_FF), BF16),
                        pltpu.SemaphoreType.DMA((2,)), pltpu.SemaphoreType.DMA((2,))],
    )
    return pl.pallas_call(
        kern,
        grid_spec=grid_spec,
        out_shape=jax.ShapeDtypeStruct((t * TOP_K + 2 * tm, d), F32),
        compiler_params=_cparams(1),
        name="expert_ffn",
    )(tile_expert, n_used, row_tok, row_tok, row_dst, x1, wg, wu, wd, bg, bu, bd)


def _row_table_kernel(first_ref, nvalid_ref, a_ref, tok_ref, dst_ref, *, tm, t, n_tiles):
    n_assign = TOP_K * t
    rows_per_tile = tm // LANES
    lane = lax.broadcasted_iota(I32, (rows_per_tile, LANES), 1)
    pos = lax.broadcasted_iota(I32, (rows_per_tile, LANES), 0) * LANES + lane

    def body(i, carry):
        first = first_ref[i]
        q = lax.shift_right_logical(first, 7)
        m = first & (LANES - 1)
        rows = a_ref[pl.ds(q, rows_per_tile + 1), :]
        rolled = pltpu.roll(rows, lax.rem(LANES - m, LANES), 1)
        a = jnp.where(lane + m < LANES, rolled[:rows_per_tile], rolled[1:])
        valid = pos < nvalid_ref[i]
        tok = a
        for k in range(1, TOP_K):
            tok = tok - jnp.where(a >= k * t, t, 0)
        out = pl.ds(i * rows_per_tile, rows_per_tile)
        tok_ref[out, :] = jnp.where(valid, tok, 0)
        dst_ref[out, :] = jnp.where(valid, a, n_assign + lax.rem(i, 2) * tm + pos)
        return carry

    lax.fori_loop(0, n_tiles, body, 0)


def _route(top_idx, count_parts, tm):
    t = top_idx.shape[1]
    n_assign = t * TOP_K
    n_tiles = n_assign // tm + N_EXPERTS
    e_flat = top_idx.reshape(-1)
    ids = jnp.arange(n_assign, dtype=I32)
    _, a_sorted = lax.sort((e_flat, ids), num_keys=1, is_stable=True)
    counts = jnp.sum(count_parts, axis=1)
    padded = ((counts + tm - 1) // tm) * tm
    start = jnp.cumsum(counts) - counts
    pend = jnp.cumsum(padded)
    pstart = pend - padded
    n_used = (pend[-1] // tm).astype(I32)
    tile = jnp.arange(n_tiles, dtype=I32)
    te = jnp.sum((tile[:, None] * tm >= pend[None, :]).astype(I32), axis=1)
    last = jnp.sum(((n_used - 1) * tm >= pend).astype(I32))
    te = jnp.minimum(te, last)
    in_expert = tile * tm - pstart[te]
    n_valid = jnp.where(tile < n_used, jnp.clip(counts[te] - in_expert, 0, tm), 0).astype(I32)
    first = jnp.where(tile < n_used, start[te] + in_expert, 0).astype(I32)

    a2 = jnp.pad(a_sorted, (0, 8 * LANES)).reshape(-1, LANES)
    rows_out = n_tiles * tm // LANES
    kern = functools.partial(_row_table_kernel, tm=tm, t=t, n_tiles=n_tiles)
    table = jax.ShapeDtypeStruct((rows_out, LANES), I32)
    row_tok, row_dst = pl.pallas_call(
        kern,
        grid_spec=pltpu.PrefetchScalarGridSpec(
            num_scalar_prefetch=2,
            grid=(1,),
            in_specs=[pl.BlockSpec(a2.shape, lambda i, f, nv: (0, 0))],
            out_specs=[pl.BlockSpec((rows_out, LANES), lambda i, f, nv: (0, 0))] * 2,
        ),
        out_shape=[table, table],
        compiler_params=_cparams(1),
        name="row_tables",
    )(first, n_valid, a2)
    return (te, n_used.reshape(1), row_tok.reshape(n_tiles, 1, tm), row_dst.reshape(n_tiles, 1, tm))


def _combine_kernel(y0_ref, y1_ref, y2_ref, y3_ref, x_ref, gate_ref, g_ref, b_ref, o_ref, *, alpha):
    tm = x_ref.shape[0]
    cols = []
    for p in range(tm // LANES):
        piece = jnp.concatenate([gate_ref[:, p * LANES:(p + 1) * LANES],
                                 jnp.zeros((LANES - gate_ref.shape[0], LANES), F32)], axis=0)
        cols.append(jnp.transpose(piece))
    gates = [jnp.concatenate([c[:, k:k + 1] for c in cols], axis=0) for k in range(TOP_K)]
    ffn = ((gates[0] * y0_ref[...] + gates[1] * y1_ref[...])
           + (gates[2] * y2_ref[...] + gates[3] * y3_ref[...]))
    o_ref[...] = _layer_norm(alpha * x_ref[...] + ffn, g_ref[...], b_ref[...])


def _combine_ln(y4, x1, gates, g, b, alpha):
    t, d = x1.shape
    tm = min(ROW_TILE, t)
    nb = t // tm
    kern = functools.partial(_combine_kernel, alpha=alpha)
    slot_specs = [pl.BlockSpec((tm, d), functools.partial(lambda i, k: (k * nb + i, 0), k=k))
                  for k in range(TOP_K)]
    return pl.pallas_call(
        kern,
        grid=(nb,),
        in_specs=slot_specs + [pl.BlockSpec((tm, d), lambda i: (i, 0)),
                               pl.BlockSpec((gates.shape[0], tm), lambda i: (0, i)),
                               pl.BlockSpec((1, d), lambda i: (0, 0)),
                               pl.BlockSpec((1, d), lambda i: (0, 0))],
        out_specs=pl.BlockSpec((tm, d), lambda i: (i, 0)),
        out_shape=jax.ShapeDtypeStruct((t, d), F32),
        compiler_params=_cparams(1),
        name="combine_ln",
    )(y4, y4, y4, y4, x1, gates, g, b)


def _deinterleave_kernel(w_ref, g_ref, u_ref):
    half = MXU_TILE // 2
    src = lax.broadcasted_iota(I32, (MXU_TILE, MXU_TILE), 0)
    dst = lax.broadcasted_iota(I32, (MXU_TILE, MXU_TILE), 1)
    perm = jnp.where(src == jnp.where(dst < half, 2 * dst, 2 * (dst - half) + 1), 1.0, 0.0).astype(BF16)
    for c in range(w_ref.shape[1] // MXU_TILE):
        blk = w_ref[:, c * MXU_TILE:(c + 1) * MXU_TILE].astype(BF16)
        sp = jnp.dot(blk, perm, preferred_element_type=F32)
        g_ref[:, c * half:(c + 1) * half] = sp[:, :half].astype(BF16)
        u_ref[:, c * half:(c + 1) * half] = sp[:, half:].astype(BF16)


def _deinterleave(w_gu):
    n, d, f2 = w_gu.shape
    tr = min(1024, d)
    out = jax.ShapeDtypeStruct((n, d, f2 // 2), BF16)
    return pl.pallas_call(
        _deinterleave_kernel,
        grid=(n, d // tr),
        in_specs=[pl.BlockSpec((None, tr, f2), lambda e, r: (e, r, 0))],
        out_specs=[pl.BlockSpec((None, tr, f2 // 2), lambda e, r: (e, r, 0))] * 2,
        out_shape=[out, out],
        compiler_params=_cparams(2),
        name="split_gate_up",
    )(w_gu)


def _rotary_tables(seq):
    half = RET_HEAD_DIM // 2
    inv_freq = ROPE_BASE ** (-jnp.linspace(0.0, 1.0, half, dtype=F32))
    ang = jnp.arange(seq).astype(F32)[:, None] * inv_freq[None, :]
    cos, sin = jnp.cos(ang), jnp.sin(ang)
    return (jnp.concatenate([cos, cos], axis=-1), jnp.concatenate([-sin, sin], axis=-1))


def _decay_tables():
    c = RET_CHUNK
    lg = jnp.log1p(-jnp.exp2(-5.0 - jnp.arange(RET_HEADS, dtype=F32)))
    i = jnp.arange(c, dtype=F32)
    diff = i[:, None] - i[None, :]
    intra = jnp.where(diff[None] >= 0, jnp.exp(jnp.maximum(diff, 0.0)[None] * lg[:, None, None]), 0.0)
    q_dec = jnp.exp((i + 1.0)[None, :] * lg[:, None])[..., None]
    k_dec = jnp.exp((c - 1.0 - i)[None, :] * lg[:, None])[..., None]
    c_dec = jnp.exp(c * lg)[:, None, None]
    full = (RET_HEADS, c, c)
    return jnp.stack([intra.astype(F32), jnp.broadcast_to(q_dec, full),
                      jnp.broadcast_to(k_dec, full), jnp.broadcast_to(c_dec, full)])


def _split_w_in(w_in):
    pts = np.cumsum((0,) + IN_SIZES)
    a = w_in[..., pts[0]:pts[4]]
    idx = w_in[..., pts[4]:pts[6]]
    idx = jnp.pad(idx, ((0, 0), (0, 0), (0, LANES - idx.shape[-1])))
    rest = jnp.concatenate([w_in[..., pts[6]:], idx], axis=-1)
    return a.astype(BF16), rest.astype(BF16)


def _cast_kernel(x_ref, o_ref):
    o_ref[...] = x_ref[...].astype(o_ref.dtype)


def _to_bf16(w):
    n, r, c = w.shape
    blk = (2 if n % 2 == 0 else 1, r, c)
    return pl.pallas_call(
        _cast_kernel,
        grid=(n // blk[0],),
        in_specs=[pl.BlockSpec(blk, lambda i: (i, 0, 0))],
        out_specs=pl.BlockSpec(blk, lambda i: (i, 0, 0)),
        out_shape=jax.ShapeDtypeStruct(w.shape, BF16),
        compiler_params=_cparams(1),
        name="cast_bf16",
    )(w)


def _v_transposed(a_pack, batch, seq):
    v = a_pack[:, ATTN_W + KV_W:ATTN_W + 2 * KV_W].reshape(batch, seq, ATTN_KV_HEADS, HEAD_DIM)
    vt = jnp.transpose(v, (0, 2, 3, 1))
    ones = jnp.ones((batch, ATTN_KV_HEADS, BF16_ROWS, seq), BF16)
    return jnp.concatenate([vt, ones], axis=2)


def kernel(x, w_in, conv_w, w_out, ln1_g, ln1_b, router_w, router_b,
           w_gu, b_gu, w_down, b_down, ln2_g, ln2_b):
    batch, seq, d = x.shape
    depth = w_in.shape[0]
    t = batch * seq
    alpha = (2 * depth) ** 0.25
    topk = min(IDX_TOPK_MAX, seq // 4)
    tm_e = EXPERT_TILE

    wa, wr = _split_w_in(w_in)
    idx_blk = (wr.shape[-1] - LANES) // LANES
    w_out_b = w_out.astype(BF16)
    rw_t = jnp.swapaxes(router_w, 1, 2).astype(BF16)
    n_exp = depth * N_EXPERTS
    wg, wu = _deinterleave(w_gu.reshape(n_exp, d, 2 * D_FF))
    wd = _to_bf16(w_down.reshape(n_exp, D_FF, d))
    bg = b_gu[..., 0::2].reshape(n_exp, 1, D_FF)
    bu = b_gu[..., 1::2].reshape(n_exp, 1, D_FF)
    bd = b_down.reshape(n_exp, 1, d)
    cw = jnp.pad(conv_w, ((0, 0), (0, 8 - CONV_WIDTH), (0, 0)))
    cos_t, sin_t = _rotary_tables(seq)
    dec = _decay_tables()

    xf = x.reshape(t, d)
    for l in range(depth):
        a_pack = _matmul(xf, wa[l], min(512, t), ATTN_W, BF16, "in_proj_attn",
                         first_block_scale=HEAD_DIM ** -0.5 * LOG2_E)
        r_pack = _matmul(xf, wr[l], min(ROW_TILE, t), wr.shape[-1], F32, "in_proj_rest")
        attn = _attention(a_pack, r_pack, idx_blk, _v_transposed(a_pack, batch, seq), batch, seq, topk)
        conv, ret = _retconv(r_pack, cw[l], cos_t, sin_t, dec, batch, seq)
        x1, top_idx, gates, count_parts = _outproj_ln_router(
            attn, conv, ret, xf, w_out_b[l], ln1_g[l][None], ln1_b[l][None],
            rw_t[l], router_b[l][:, None], alpha)
        te, n_used, row_tok, row_dst = _route(top_idx, count_parts, tm_e)
        y4 = _experts(x1, te, n_used, row_tok, row_dst, wg, wu, wd, bg, bu, bd, l, tm_e)
        xf = _combine_ln(y4, x1, gates, ln2_g[l][None], ln2_b[l][None], alpha)
    return xf.reshape(batch, seq, d)
```
